```python
import jax, jax.numpy as jnp
from jax import lax
import numpy as np

D_MODEL = 1024
BATCH = 4
SEQ = 8192
DEPTH = 4

N_MIXERS = 2
N_HEADS = 16
HEAD_DIM = D_MODEL // N_HEADS
ATTN_WIDTH = N_HEADS * HEAD_DIM
Q_BLOCK = 128
POOL_WINDOWS = (2, 4, 8, 16)
N_POOL_GROUPS = len(POOL_WINDOWS)
POOL_GROUP_DIM = D_MODEL // N_POOL_GROUPS
D_FF = ((8 * D_MODEL // 3 + 255) // 256) * 256
N_ATTN_LAYERS = (DEPTH + 1) // 2
N_POOL_LAYERS = DEPTH // 2
EPS = 1e-6

kernel_name = "stickbreak_pool_interleaved_trunk"


def rmsnorm(x, g):
    xf = x.astype(jnp.float32)
    y = xf * lax.rsqrt(jnp.mean(xf * xf, axis=-1, keepdims=True) + EPS)
    return (y * g.astype(jnp.float32)).astype(x.dtype)


def stick_breaking_attention(h, w_qkv, q_gain, k_gain, w_o):
    B, S, _ = h.shape
    qkv = jnp.einsum('bsd,de->bse', h, w_qkv).reshape(B, S, 3, N_HEADS, HEAD_DIM)
    q = rmsnorm(qkv[:, :, 0], q_gain).transpose(0, 2, 1, 3)
    k = rmsnorm(qkv[:, :, 1], k_gain).transpose(0, 2, 1, 3)
    v = qkv[:, :, 2].transpose(0, 2, 1, 3)
    scale = HEAD_DIM ** -0.5
    n_blocks = S // Q_BLOCK
    q_blocks = q.reshape(B, N_HEADS, n_blocks, Q_BLOCK, HEAD_DIM).transpose(2, 0, 1, 3, 4)
    key_pos = jnp.arange(S)

    def one_block(args):
        q_blk, blk_idx = args
        q_pos = blk_idx * Q_BLOCK + jnp.arange(Q_BLOCK)
        z = jnp.einsum('bhqd,bhkd->bhqk', q_blk, k).astype(jnp.float32) * scale
        mask = key_pos[None, :] < q_pos[:, None]
        log_1m = jnp.where(mask, jax.nn.log_sigmoid(-z), 0.0)
        tail = lax.cumsum(log_1m, axis=3, reverse=True) - log_1m
        a = jnp.where(mask, jnp.exp(jax.nn.log_sigmoid(z) + tail), 0.0)
        return jnp.einsum('bhqk,bhkd->bhqd', a.astype(v.dtype), v)

    o = lax.map(one_block, (q_blocks, jnp.arange(n_blocks)))
    o = o.transpose(1, 3, 0, 2, 4).reshape(B, S, ATTN_WIDTH)
    return jnp.einsum('bse,ed->bsd', o, w_o)


def multiscale_pool_mixer(h, w_grp, scale):
    B, S, D = h.shape
    hf = h.astype(jnp.float32).reshape(B, S, N_POOL_GROUPS, POOL_GROUP_DIM)
    csum = jnp.cumsum(hf, axis=1)
    pos = jnp.arange(S)
    pooled = []
    for g, w in enumerate(POOL_WINDOWS):
        cg = csum[:, :, g]
        lagged = jnp.pad(cg, ((0, 0), (w, 0), (0, 0)))[:, :S]
        count = jnp.minimum(pos + 1, w).astype(jnp.float32)[None, :, None]
        pooled.append((cg - lagged) / count - hf[:, :, g])
    p = jnp.stack(pooled, axis=2).astype(h.dtype)
    y = jnp.einsum('bsgc,gce->bsge', p, w_grp).reshape(B, S, D)
    return y * scale


def swiglu_ffn(h, w_gu, w_down):
    gu = jnp.einsum('bsd,df->bsf', h, w_gu)
    gate, up = gu[..., :D_FF], gu[..., D_FF:]
    return jnp.einsum('bsf,fd->bsd', jax.nn.silu(gate) * up, w_down)


def setup_inputs(seed: int = 0) -> dict:
    key = jax.random.key(seed)
    ks = jax.random.split(key, 12)
    f32 = jnp.float32
    x = jax.random.normal(ks[0], (BATCH, SEQ, D_MODEL), f32)
    mix_norm = 1.0 + 0.02 * jax.random.normal(ks[1], (DEPTH, D_MODEL), f32)
    ffn_norm = 1.0 + 0.02 * jax.random.normal(ks[2], (DEPTH, D_MODEL), f32)
    attn_w_qkv = jax.random.normal(ks[3], (N_ATTN_LAYERS, D_MODEL, 3 * ATTN_WIDTH), f32) * D_MODEL ** -0.5
    attn_q_norm = 1.0 + 0.02 * jax.random.normal(ks[4], (N_ATTN_LAYERS, HEAD_DIM), f32)
    attn_k_norm = 1.0 + 0.02 * jax.random.normal(ks[5], (N_ATTN_LAYERS, HEAD_DIM), f32)
    attn_w_o = jax.random.normal(ks[6], (N_ATTN_LAYERS, ATTN_WIDTH, D_MODEL), f32) * ATTN_WIDTH ** -0.5
    pool_w = jax.random.normal(ks[7], (N_POOL_LAYERS, N_POOL_GROUPS, POOL_GROUP_DIM, POOL_GROUP_DIM), f32) * POOL_GROUP_DIM ** -0.5
    pool_scale = 1.0 + 0.02 * jax.random.normal(ks[8], (N_POOL_LAYERS, D_MODEL), f32)
    ffn_w_gu = jax.random.normal(ks[9], (DEPTH, D_MODEL, 2 * D_FF), f32) * D_MODEL ** -0.5
    ffn_w_down = jax.random.normal(ks[10], (DEPTH, D_FF, D_MODEL), f32) * D_FF ** -0.5
    return {"x": x, "mix_norm": mix_norm, "ffn_norm": ffn_norm,
            "attn_w_qkv": attn_w_qkv, "attn_q_norm": attn_q_norm, "attn_k_norm": attn_k_norm,
            "attn_w_o": attn_w_o, "pool_w": pool_w, "pool_scale": pool_scale,
            "ffn_w_gu": ffn_w_gu, "ffn_w_down": ffn_w_down}


def reference(x, mix_norm, ffn_norm, attn_w_qkv, attn_q_norm, attn_k_norm, attn_w_o,
              pool_w, pool_scale, ffn_w_gu, ffn_w_down):
    for i in range(DEPTH):
        j = i // N_MIXERS
        h = rmsnorm(x, mix_norm[i])
        if i % N_MIXERS == 0:
            x = x + stick_breaking_attention(h, attn_w_qkv[j], attn_q_norm[j], attn_k_norm[j], attn_w_o[j])
        else:
            x = x + multiscale_pool_mixer(h, pool_w[j], pool_scale[j])
        h = rmsnorm(x, ffn_norm[i])
        x = x + swiglu_ffn(h, ffn_w_gu[i], ffn_w_down[i])
    return x
```

```python
import functools

import jax
import jax.numpy as jnp
from jax import lax
from jax.experimental import pallas as pl
from jax.experimental.pallas import tpu as pltpu

D_MODEL = 1024
N_HEADS = 16
HEAD_DIM = 64
D_FF = 2816
POOL_WINDOWS = (2, 4, 8, 16)
POOL_GROUP_DIM = D_MODEL // len(POOL_WINDOWS)
POOL_HALO = 16
EPS = 1e-6

Q_BLOCK = 128
LANES = 128
SUBLANES = 8
MXU_DIM = 256
VMEM_LIMIT_BYTES = 56 * 1024 * 1024

TOKEN_TILE = 512
ATTN_TILE = 256
FF_CHUNK = 704

BF16 = jnp.bfloat16
F32 = jnp.float32


def _dot(a, b):
    return jnp.dot(a, b, preferred_element_type=F32)


def _rmsnorm_rows(x, gain):
    ms = jnp.mean(x * x, axis=-1, keepdims=True)
    return x * lax.rsqrt(ms + EPS) * gain


def _split_bf16(x):
    hi = x.astype(BF16)
    lo = (x - hi.astype(F32)).astype(BF16)
    return hi, lo


def _const_spec(shape):
    zeros = (0,) * len(shape)
    return pl.BlockSpec(shape, lambda *_: zeros)


def _head_meansq(t, seg):
    sq = t * t
    outs = []
    for c in range(D_MODEL // MXU_DIM):
        hi, lo = _split_bf16(sq[:, c * MXU_DIM:(c + 1) * MXU_DIM])
        outs.append(_dot(hi, seg) + _dot(lo, seg))
    return jnp.concatenate(outs, axis=-1) * (1.0 / HEAD_DIM)


def _qkv_kernel(x_ref, g_ref, w_ref, qg_ref, kg_ref, seg_ref, q_ref, kt_ref, v_ref):
    h = _rmsnorm_rows(x_ref[0], g_ref[...]).astype(BF16)
    seg = seg_ref[...]
    q = _dot(h, w_ref[:, 0:D_MODEL])
    q = q * lax.rsqrt(_head_meansq(q, seg) + EPS) * qg_ref[...]
    q_ref[0] = q.astype(BF16)
    k = _dot(h, w_ref[:, D_MODEL:2 * D_MODEL])
    k = k * lax.rsqrt(_head_meansq(k, seg) + EPS) * kg_ref[...]
    for j in range(TOKEN_TILE // ATTN_TILE):
        kt_ref[0, j] = k[j * ATTN_TILE:(j + 1) * ATTN_TILE, :].T.astype(BF16)
    v_ref[0] = _dot(h, w_ref[:, 2 * D_MODEL:3 * D_MODEL]).astype(BF16)


def _qkv_call(x, gain, w_qkv, q_gain, k_gain, seg):
    B, S, D = x.shape
    n_kblk = TOKEN_TILE // ATTN_TILE
    return pl.pallas_call(
        _qkv_kernel,
        grid=(B, S // TOKEN_TILE),
        in_specs=[
            pl.BlockSpec((1, TOKEN_TILE, D), lambda b, i: (b, i, 0)),
            _const_spec((1, D)),
            _const_spec((D, 3 * D)),
            _const_spec((1, D)),
            _const_spec((1, D)),
            _const_spec((MXU_DIM, MXU_DIM)),
        ],
        out_specs=[
            pl.BlockSpec((1, TOKEN_TILE, D), lambda b, i: (b, i, 0)),
            pl.BlockSpec((1, n_kblk, D, ATTN_TILE), lambda b, i: (b, i, 0, 0)),
            pl.BlockSpec((1, TOKEN_TILE, D), lambda b, i: (b, i, 0)),
        ],
        out_shape=[
            jax.ShapeDtypeStruct((B, S, D), BF16),
            jax.ShapeDtypeStruct((B, S // ATTN_TILE, D, ATTN_TILE), BF16),
            jax.ShapeDtypeStruct((B, S, D), BF16),
        ],
        compiler_params=pltpu.CompilerParams(
            dimension_semantics=("arbitrary", "arbitrary"),
            vmem_limit_bytes=VMEM_LIMIT_BYTES),
        name="qkv_proj",
    )(x, gain, w_qkv, q_gain, k_gain, seg)


def _log_one_minus_sigmoid(z):
    return -(jnp.maximum(z, 0.0) + jnp.log(1.0 + jnp.exp(-jnp.abs(z))))


def _block_weights(q, kt, tri, carry, mask):
    z = _dot(q, kt)
    l = _log_one_minus_sigmoid(z)
    if mask is not None:
        l = jnp.where(mask, l, 0.0)
    hi, lo = _split_bf16(l)
    c = _dot(hi, tri) + _dot(lo, tri) + carry
    a = jnp.exp(z + c)
    if mask is not None:
        a = jnp.where(mask, a, 0.0)
    return a.astype(BF16), c[:, 0:1]


def _attn_kernel(q_ref, kt_ref, v_ref, tri_ref, o_ref):
    qi = pl.program_id(2)
    T = ATTN_TILE
    q = q_ref[0]
    lane = lax.broadcasted_iota(jnp.int32, (T, LANES), 1)
    first = lane < HEAD_DIM
    zero = jnp.zeros_like(q)
    q_heads = (jnp.where(first, q, zero), jnp.where(first, zero, q))
    tri = tri_ref[...]

    row = lax.broadcasted_iota(jnp.int32, (T, T), 0)
    col = lax.broadcasted_iota(jnp.int32, (T, T), 1)
    diag_mask = col < row

    def visit(kb, state, mask):
        kt = kt_ref[0, kb]
        v = v_ref[0, pl.ds(pl.multiple_of(kb * T, T), T), :]
        new = []
        for qh, (acc, carry) in zip(q_heads, state):
            a, carry = _block_weights(qh, kt, tri, carry, mask)
            new.append((acc + _dot(a, v), carry))
        return tuple(new)

    init = tuple((jnp.zeros((T, LANES), F32), jnp.zeros((T, 1), F32)) for _ in q_heads)
    state = visit(qi, init, diag_mask)
    state = lax.fori_loop(0, qi, lambda i, s: visit(qi - 1 - i, s, None), state)
    o_ref[0] = jnp.where(first, state[0][0], state[1][0])


def _attn_call(q, kt, v, tri):
    B, S, D = q.shape
    T = ATTN_TILE
    return pl.pallas_call(
        _attn_kernel,
        grid=(B, D // LANES, S // T),
        in_specs=[
            pl.BlockSpec((1, T, LANES), lambda b, p, i: (b, i, p)),
            pl.BlockSpec((1, S // T, LANES, T), lambda b, p, i: (b, 0, p, 0)),
            pl.BlockSpec((1, S, LANES), lambda b, p, i: (b, 0, p)),
            _const_spec((T, T)),
        ],
        out_specs=pl.BlockSpec((1, T, LANES), lambda b, p, i: (b, i, p)),
        out_shape=jax.ShapeDtypeStruct((B, S, D), F32),
        compiler_params=pltpu.CompilerParams(
            dimension_semantics=("arbitrary", "arbitrary", "arbitrary"),
            vmem_limit_bytes=VMEM_LIMIT_BYTES),
        name="stickbreak_attn",
    )(q, kt, v, tri)


def _ffn_tile(x1, g_ref, wgu_ref, wd_ref, act_ref):
    h = _rmsnorm_rows(x1, g_ref[...]).astype(BF16)
    for c in range(D_FF // FF_CHUNK):
        lo = c * FF_CHUNK
        gate = _dot(h, wgu_ref[:, lo:lo + FF_CHUNK])
        up = _dot(h, wgu_ref[:, D_FF + lo:D_FF + lo + FF_CHUNK])
        act = gate / (1.0 + jnp.exp(-gate)) * up
        act_ref[:, lo:lo + FF_CHUNK] = act.astype(BF16)
    return x1 + _dot(act_ref[...], wd_ref[...])


def _attn_tail_kernel(x_ref, o_ref, wo_ref, g_ref, wgu_ref, wd_ref, out_ref, act_ref):
    rows_per_block = o_ref.shape[2]
    o = jnp.concatenate([o_ref[0, :, r, :] for r in range(rows_per_block)], axis=0)
    x1 = x_ref[0] + _dot(o.astype(BF16), wo_ref[...])
    out_ref[0] = _ffn_tile(x1, g_ref, wgu_ref, wd_ref, act_ref)


def _attn_tail_call(x, o, w_o, gain, w_gu, w_down):
    B, S, D = x.shape
    n_qblk = S // Q_BLOCK
    assert TOKEN_TILE % n_qblk == 0 and (TOKEN_TILE // n_qblk) % SUBLANES == 0
    row_spec = pl.BlockSpec((1, TOKEN_TILE, D), lambda b, i: (b, i, 0))
    return pl.pallas_call(
        _attn_tail_kernel,
        grid=(B, S // TOKEN_TILE),
        in_specs=[row_spec,
                  pl.BlockSpec((1, n_qblk, TOKEN_TILE // n_qblk, D), lambda b, i: (b, 0, i, 0)),
                  _const_spec((D, D)), _const_spec((1, D)),
                  _const_spec((D, 2 * D_FF)), _const_spec((D_FF, D))],
        out_specs=row_spec,
        out_shape=jax.ShapeDtypeStruct((B, S, D), F32),
        scratch_shapes=[pltpu.VMEM((TOKEN_TILE, D_FF), BF16)],
        compiler_params=pltpu.CompilerParams(
            dimension_semantics=("arbitrary", "arbitrary"),
            vmem_limit_bytes=VMEM_LIMIT_BYTES),
        name="outproj_ffn",
    )(x, o.reshape(B, n_qblk, Q_BLOCK, D), w_o, gain, w_gu, w_down)


def _pool_tail_kernel(x_ref, halo_ref, mg_ref, pw_ref, ps_ref, g_ref, wgu_ref, wd_ref,
                      out_ref, hist_ref, act_ref):
    si = pl.program_id(1)
    T = TOKEN_TILE
    x = x_ref[0]
    h = _rmsnorm_rows(x, mg_ref[...])
    h_halo = _rmsnorm_rows(halo_ref[0], mg_ref[...])
    hist_ref[0:POOL_HALO, :] = jnp.where(si > 0, h_halo, 0.0)
    hist_ref[POOL_HALO:POOL_HALO + T, :] = h

    pos = si * T + lax.broadcasted_iota(jnp.int32, (T, 1), 0)
    mixed = []
    for g, w in enumerate(POOL_WINDOWS):
        lo = g * POOL_GROUP_DIM
        acc = h[:, lo:lo + POOL_GROUP_DIM]
        for back in range(1, w):
            acc = acc + hist_ref[POOL_HALO - back:POOL_HALO - back + T, lo:lo + POOL_GROUP_DIM]
        count = jnp.minimum(pos + 1, w).astype(F32)
        pooled = acc / count - h[:, lo:lo + POOL_GROUP_DIM]
        mixed.append(_dot(pooled.astype(BF16), pw_ref[g]))
    x1 = x + jnp.concatenate(mixed, axis=-1) * ps_ref[...]
    out_ref[0] = _ffn_tile(x1, g_ref, wgu_ref, wd_ref, act_ref)


def _pool_tail_call(x, mix_gain, pool_w, pool_scale, gain, w_gu, w_down):
    B, S, D = x.shape
    T = TOKEN_TILE
    halo_blocks = T // POOL_HALO
    return pl.pallas_call(
        _pool_tail_kernel,
        grid=(B, S // T),
        in_specs=[
            pl.BlockSpec((1, T, D), lambda b, i: (b, i, 0)),
            pl.BlockSpec((1, POOL_HALO, D),
                         lambda b, i: (b, jnp.maximum(i * halo_blocks - 1, 0), 0)),
            _const_spec((1, D)),
            _const_spec(pool_w.shape),
            _const_spec((1, D)),
            _const_spec((1, D)),
            _const_spec((D, 2 * D_FF)),
            _const_spec((D_FF, D)),
        ],
        out_specs=pl.BlockSpec((1, T, D), lambda b, i: (b, i, 0)),
        out_shape=jax.ShapeDtypeStruct((B, S, D), F32),
        scratch_shapes=[pltpu.VMEM((POOL_HALO + T, D), F32),
                        pltpu.VMEM((T, D_FF), BF16)],
        compiler_params=pltpu.CompilerParams(
            dimension_semantics=("arbitrary", "arbitrary"),
            vmem_limit_bytes=VMEM_LIMIT_BYTES),
        name="pool_ffn",
    )(x, x, mix_gain, pool_w, pool_scale, gain, w_gu, w_down)


def kernel(x, mix_norm, ffn_norm, attn_w_qkv, attn_q_norm, attn_k_norm, attn_w_o,
           pool_w, pool_scale, ffn_w_gu, ffn_w_down):
    B, S, D = x.shape
    depth = mix_norm.shape[0]
    assert D == D_MODEL and S % TOKEN_TILE == 0 and TOKEN_TILE % ATTN_TILE == 0

    idx = jnp.arange(MXU_DIM)
    seg = (idx[:, None] // HEAD_DIM == idx[None, :] // HEAD_DIM).astype(BF16)
    kidx = jnp.arange(ATTN_TILE)
    tri = (kidx[:, None] >= kidx[None, :]).astype(BF16)

    for i in range(depth):
        j = i // 2
        gain = ffn_norm[i][None, :]
        w_gu = ffn_w_gu[i].astype(BF16)
        w_down = ffn_w_down[i].astype(BF16)
        if i % 2 == 0:
            q_gain = jnp.tile(attn_q_norm[j], N_HEADS)[None, :] * (HEAD_DIM ** -0.5)
            k_gain = jnp.tile(attn_k_norm[j], N_HEADS)[None, :]
            q, kt, v = _qkv_call(x, mix_norm[i][None, :], attn_w_qkv[j].astype(BF16),
                                 q_gain, k_gain, seg)
            o = _attn_call(q, kt, v, tri)
            x = _attn_tail_call(x, o, attn_w_o[j].astype(BF16), gain, w_gu, w_down)
        else:
            x = _pool_tail_call(x, mix_norm[i][None, :], pool_w[j].astype(BF16),
                                pool_scale[j][None, :], gain, w_gu, w_down)
    return x
```

```python
import functools

import jax
import jax.numpy as jnp
from jax import lax
from jax.experimental import pallas as pl
from jax.experimental.pallas import tpu as pltpu

D_MODEL = 1024
N_HEADS = 16
HEAD_DIM = 64
D_FF = 2816
POOL_WINDOWS = (2, 4, 8, 16)
POOL_GROUP_DIM = D_MODEL // len(POOL_WINDOWS)
POOL_HALO = 16
EPS = 1e-6

Q_BLOCK = 128
LANES = 128
SUBLANES = 8
MXU_DIM = 256
VMEM_LIMIT_BYTES = 56 * 1024 * 1024

TOKEN_TILE = 512
ATTN_TILE = 256
LOG_WEIGHT_FLOOR = -105.0
FF_CHUNK = 704

BF16 = jnp.bfloat16
F32 = jnp.float32


def _dot(a, b):
    return jnp.dot(a, b, preferred_element_type=F32)


def _rmsnorm_rows(x, gain):
    ms = jnp.mean(x * x, axis=-1, keepdims=True)
    return x * lax.rsqrt(ms + EPS) * gain


def _split_bf16(x):
    hi = x.astype(BF16)
    lo = (x - hi.astype(F32)).astype(BF16)
    return hi, lo


def _const_spec(shape):
    zeros = (0,) * len(shape)
    return pl.BlockSpec(shape, lambda *_: zeros)


def _head_meansq(t, seg):
    sq = t * t
    outs = []
    for c in range(D_MODEL // MXU_DIM):
        hi, lo = _split_bf16(sq[:, c * MXU_DIM:(c + 1) * MXU_DIM])
        outs.append(_dot(hi, seg) + _dot(lo, seg))
    return jnp.concatenate(outs, axis=-1) * (1.0 / HEAD_DIM)


def _qkv_kernel(x_ref, g_ref, w_ref, qg_ref, kg_ref, seg_ref, q_ref, kt_ref, v_ref):
    h = _rmsnorm_rows(x_ref[0], g_ref[...]).astype(BF16)
    seg = seg_ref[...]
    q = _dot(h, w_ref[:, 0:D_MODEL])
    q = q * lax.rsqrt(_head_meansq(q, seg) + EPS) * qg_ref[...]
    q_ref[0] = q.astype(BF16)
    k = _dot(h, w_ref[:, D_MODEL:2 * D_MODEL])
    k = k * lax.rsqrt(_head_meansq(k, seg) + EPS) * kg_ref[...]
    for j in range(TOKEN_TILE // ATTN_TILE):
        kt_ref[0, j] = k[j * ATTN_TILE:(j + 1) * ATTN_TILE, :].T.astype(BF16)
    v_ref[0] = _dot(h, w_ref[:, 2 * D_MODEL:3 * D_MODEL]).astype(BF16)


def _qkv_call(x, gain, w_qkv, q_gain, k_gain, seg):
    B, S, D = x.shape
    n_kblk = TOKEN_TILE // ATTN_TILE
    return pl.pallas_call(
        _qkv_kernel,
        grid=(B, S // TOKEN_TILE),
        in_specs=[
            pl.BlockSpec((1, TOKEN_TILE, D), lambda b, i: (b, i, 0)),
            _const_spec((1, D)),
            _const_spec((D, 3 * D)),
            _const_spec((1, D)),
            _const_spec((1, D)),
            _const_spec((MXU_DIM, MXU_DIM)),
        ],
        out_specs=[
            pl.BlockSpec((1, TOKEN_TILE, D), lambda b, i: (b, i, 0)),
            pl.BlockSpec((1, n_kblk, D, ATTN_TILE), lambda b, i: (b, i, 0, 0)),
            pl.BlockSpec((1, TOKEN_TILE, D), lambda b, i: (b, i, 0)),
        ],
        out_shape=[
            jax.ShapeDtypeStruct((B, S, D), BF16),
            jax.ShapeDtypeStruct((B, S // ATTN_TILE, D, ATTN_TILE), BF16),
            jax.ShapeDtypeStruct((B, S, D), BF16),
        ],
        compiler_params=pltpu.CompilerParams(
            dimension_semantics=("arbitrary", "arbitrary"),
            vmem_limit_bytes=VMEM_LIMIT_BYTES),
        name="qkv_proj",
    )(x, gain, w_qkv, q_gain, k_gain, seg)


def _log_one_minus_sigmoid(z):
    return -(jnp.maximum(z, 0.0) + jnp.log(1.0 + jnp.exp(-jnp.abs(z))))


def _block_weights(q, kt, tri, carry, mask):
    z = _dot(q, kt)
    l = _log_one_minus_sigmoid(z)
    if mask is not None:
        l = jnp.where(mask, l, 0.0)
    hi, lo = _split_bf16(l)
    c = _dot(hi, tri) + _dot(lo, tri) + carry
    a = jnp.exp(z + c)
    if mask is not None:
        a = jnp.where(mask, a, 0.0)
    return a.astype(BF16), c[:, 0:1]


def _attn_kernel(q_ref, kt_ref, v_ref, tri_ref, o_ref):
    qi = pl.program_id(2)
    T = ATTN_TILE
    q = q_ref[0]
    lane = lax.broadcasted_iota(jnp.int32, (T, LANES), 1)
    first = lane < HEAD_DIM
    zero = jnp.zeros_like(q)
    q_heads = (jnp.where(first, q, zero), jnp.where(first, zero, q))
    tri = tri_ref[...]

    row = lax.broadcasted_iota(jnp.int32, (T, T), 0)
    col = lax.broadcasted_iota(jnp.int32, (T, T), 1)
    diag_mask = col < row

    def visit(kb, state, mask):
        kt = kt_ref[0, kb]
        v = v_ref[0, pl.ds(pl.multiple_of(kb * T, T), T), :]
        new = []
        for qh, (acc, carry) in zip(q_heads, state):
            a, carry = _block_weights(qh, kt, tri, carry, mask)
            new.append((acc + _dot(a, v), carry))
        return tuple(new)

    def slowest_decay(state):
        return jnp.max(jnp.maximum(state[0][1], state[1][1]))

    init = tuple((jnp.zeros((T, LANES), F32), jnp.zeros((T, 1), F32)) for _ in q_heads)
    state = visit(qi, init, diag_mask)

    def more_to_do(loop_state):
        kb, worst, _ = loop_state
        return jnp.logical_and(kb >= 0, worst > LOG_WEIGHT_FLOOR)

    def step(loop_state):
        kb, _, state = loop_state
        state = visit(kb, state, None)
        return kb - 1, slowest_decay(state), state

    _, _, state = lax.while_loop(more_to_do, step, (qi - 1, slowest_decay(state), state))
    o_ref[0] = jnp.where(first, state[0][0], state[1][0])


def _attn_call(q, kt, v, tri):
    B, S, D = q.shape
    T = ATTN_TILE
    return pl.pallas_call(
        _attn_kernel,
        grid=(B, D // LANES, S // T),
        in_specs=[
            pl.BlockSpec((1, T, LANES), lambda b, p, i: (b, i, p)),
            pl.BlockSpec((1, S // T, LANES, T), lambda b, p, i: (b, 0, p, 0)),
            pl.BlockSpec((1, S, LANES), lambda b, p, i: (b, 0, p)),
            _const_spec((T, T)),
        ],
        out_specs=pl.BlockSpec((1, T, LANES), lambda b, p, i: (b, i, p)),
        out_shape=jax.ShapeDtypeStruct((B, S, D), F32),
        compiler_params=pltpu.CompilerParams(
            dimension_semantics=("arbitrary", "arbitrary", "arbitrary"),
            vmem_limit_bytes=VMEM_LIMIT_BYTES),
        name="stickbreak_attn",
    )(q, kt, v, tri)


def _ffn_tile(x1, g_ref, wgu_ref, wd_ref, act_ref):
    h = _rmsnorm_rows(x1, g_ref[...]).astype(BF16)
    for c in range(D_FF // FF_CHUNK):
        lo = c * FF_CHUNK
        gate = _dot(h, wgu_ref[:, lo:lo + FF_CHUNK])
        up = _dot(h, wgu_ref[:, D_FF + lo:D_FF + lo + FF_CHUNK])
        act = gate / (1.0 + jnp.exp(-gate)) * up
        act_ref[:, lo:lo + FF_CHUNK] = act.astype(BF16)
    return x1 + _dot(act_ref[...], wd_ref[...])


def _attn_tail_kernel(x_ref, o_ref, wo_ref, g_ref, wgu_ref, wd_ref, out_ref, act_ref):
    rows_per_block = o_ref.shape[2]
    o = jnp.concatenate([o_ref[0, :, r, :] for r in range(rows_per_block)], axis=0)
    x1 = x_ref[0] + _dot(o.astype(BF16), wo_ref[...])
    out_ref[0] = _ffn_tile(x1, g_ref, wgu_ref, wd_ref, act_ref)


def _attn_tail_call(x, o, w_o, gain, w_gu, w_down):
    B, S, D = x.shape
    n_qblk = S // Q_BLOCK
    assert TOKEN_TILE % n_qblk == 0 and (TOKEN_TILE // n_qblk) % SUBLANES == 0
    row_spec = pl.BlockSpec((1, TOKEN_TILE, D), lambda b, i: (b, i, 0))
    return pl.pallas_call(
        _attn_tail_kernel,
        grid=(B, S // TOKEN_TILE),
        in_specs=[row_spec,
                  pl.BlockSpec((1, n_qblk, TOKEN_TILE // n_qblk, D), lambda b, i: (b, 0, i, 0)),
                  _const_spec((D, D)), _const_spec((1, D)),
                  _const_spec((D, 2 * D_FF)), _const_spec((D_FF, D))],
        out_specs=row_spec,
        out_shape=jax.ShapeDtypeStruct((B, S, D), F32),
        scratch_shapes=[pltpu.VMEM((TOKEN_TILE, D_FF), BF16)],
        compiler_params=pltpu.CompilerParams(
            dimension_semantics=("arbitrary", "arbitrary"),
            vmem_limit_bytes=VMEM_LIMIT_BYTES),
        name="outproj_ffn",
    )(x, o.reshape(B, n_qblk, Q_BLOCK, D), w_o, gain, w_gu, w_down)


def _pool_tail_kernel(x_ref, halo_ref, mg_ref, pw_ref, ps_ref, g_ref, wgu_ref, wd_ref,
                      out_ref, hist_ref, act_ref):
    si = pl.program_id(1)
    T = TOKEN_TILE
    x = x_ref[0]
    h = _rmsnorm_rows(x, mg_ref[...])
    h_halo = _rmsnorm_rows(halo_ref[0], mg_ref[...])
    hist_ref[0:POOL_HALO, :] = jnp.where(si > 0, h_halo, 0.0)
    hist_ref[POOL_HALO:POOL_HALO + T, :] = h

    pos = si * T + lax.broadcasted_iota(jnp.int32, (T, 1), 0)
    mixed = []
    for g, w in enumerate(POOL_WINDOWS):
        lo = g * POOL_GROUP_DIM
        acc = h[:, lo:lo + POOL_GROUP_DIM]
        for back in range(1, w):
            acc = acc + hist_ref[POOL_HALO - back:POOL_HALO - back + T, lo:lo + POOL_GROUP_DIM]
        count = jnp.minimum(pos + 1, w).astype(F32)
        pooled = acc / count - h[:, lo:lo + POOL_GROUP_DIM]
        mixed.append(_dot(pooled.astype(BF16), pw_ref[g]))
    x1 = x + jnp.concatenate(mixed, axis=-1) * ps_ref[...]
    out_ref[0] = _ffn_tile(x1, g_ref, wgu_ref, wd_ref, act_ref)


def _pool_tail_call(x, mix_gain, pool_w, pool_scale, gain, w_gu, w_down):
    B, S, D = x.shape
    T = TOKEN_TILE
    halo_blocks = T // POOL_HALO
    return pl.pallas_call(
        _pool_tail_kernel,
        grid=(B, S // T),
        in_specs=[
            pl.BlockSpec((1, T, D), lambda b, i: (b, i, 0)),
            pl.BlockSpec((1, POOL_HALO, D),
                         lambda b, i: (b, jnp.maximum(i * halo_blocks - 1, 0), 0)),
            _const_spec((1, D)),
            _const_spec(pool_w.shape),
            _const_spec((1, D)),
            _const_spec((1, D)),
            _const_spec((D, 2 * D_FF)),
            _const_spec((D_FF, D)),
        ],
        out_specs=pl.BlockSpec((1, T, D), lambda b, i: (b, i, 0)),
        out_shape=jax.ShapeDtypeStruct((B, S, D), F32),
        scratch_shapes=[pltpu.VMEM((POOL_HALO + T, D), F32),
                        pltpu.VMEM((T, D_FF), BF16)],
        compiler_params=pltpu.CompilerParams(
            dimension_semantics=("arbitrary", "arbitrary"),
            vmem_limit_bytes=VMEM_LIMIT_BYTES),
        name="pool_ffn",
    )(x, x, mix_gain, pool_w, pool_scale, gain, w_gu, w_down)


def kernel(x, mix_norm, ffn_norm, attn_w_qkv, attn_q_norm, attn_k_norm, attn_w_o,
           pool_w, pool_scale, ffn_w_gu, ffn_w_down):
    B, S, D = x.shape
    depth = mix_norm.shape[0]
    assert D == D_MODEL and S % TOKEN_TILE == 0 and TOKEN_TILE % ATTN_TILE == 0

    idx = jnp.arange(MXU_DIM)
    seg = (idx[:, None] // HEAD_DIM == idx[None, :] // HEAD_DIM).astype(BF16)
    kidx = jnp.arange(ATTN_TILE)
    tri = (kidx[:, None] >= kidx[None, :]).astype(BF16)

    for i in range(depth):
        j = i // 2
        gain = ffn_norm[i][None, :]
        w_gu = ffn_w_gu[i].astype(BF16)
        w_down = ffn_w_down[i].astype(BF16)
        if i % 2 == 0:
            q_gain = jnp.tile(attn_q_norm[j], N_HEADS)[None, :] * (HEAD_DIM ** -0.5)
            k_gain = jnp.tile(attn_k_norm[j], N_HEADS)[None, :]
            q, kt, v = _qkv_call(x, mix_norm[i][None, :], attn_w_qkv[j].astype(BF16),
                                 q_gain, k_gain, seg)
            o = _attn_call(q, kt, v, tri)
            x = _attn_tail_call(x, o, attn_w_o[j].astype(BF16), gain, w_gu, w_down)
        else:
            x = _pool_tail_call(x, mix_norm[i][None, :], pool_w[j].astype(BF16),
                                pool_scale[j][None, :], gain, w_gu, w_down)
    return x
```

```python
import jax
import jax.numpy as jnp
from jax import lax
from jax.experimental import pallas as pl
from jax.experimental.pallas import tpu as pltpu

D_MODEL = 1024
N_HEADS = 16
HEAD_DIM = 64
D_FF = 2816
POOL_WINDOWS = (2, 4, 8, 16)
POOL_GROUP_DIM = D_MODEL // len(POOL_WINDOWS)
POOL_HALO = 16
EPS = 1e-6

Q_BLOCK = 128
LANES = 128
SUBLANES = 8
MXU_DIM = 256
VMEM_LIMIT_BYTES = 56 * 1024 * 1024

TOKEN_TILE = 512
ATTN_TILE = 256
LOG2_WEIGHT_FLOOR = -152.0
LOG2_E = 1.4426950408889634
SIGN_BIT = 0x80000000
FF_CHUNK = 704

BF16 = jnp.bfloat16
F32 = jnp.float32


def _dot(a, b):
    return jnp.dot(a, b, preferred_element_type=F32)


def _rmsnorm_rows(x, gain):
    ms = jnp.mean(x * x, axis=-1, keepdims=True)
    return x * lax.rsqrt(ms + EPS) * gain


def _split_bf16(x):
    hi = x.astype(BF16)
    lo = (x - hi.astype(F32)).astype(BF16)
    return hi, lo


def _const_spec(shape):
    zeros = (0,) * len(shape)
    return pl.BlockSpec(shape, lambda *_: zeros)


def _head_meansq(t, seg):
    sq = t * t
    outs = []
    for c in range(D_MODEL // MXU_DIM):
        hi, lo = _split_bf16(sq[:, c * MXU_DIM:(c + 1) * MXU_DIM])
        outs.append(_dot(hi, seg) + _dot(lo, seg))
    return jnp.concatenate(outs, axis=-1) * (1.0 / HEAD_DIM)


def _qkv_kernel(x_ref, g_ref, w_ref, qg_ref, kg_ref, seg_ref, q_ref, kt_ref, v_ref):
    h = _rmsnorm_rows(x_ref[0], g_ref[...]).astype(BF16)
    seg = seg_ref[...]
    q = _dot(h, w_ref[:, 0:D_MODEL])
    q = q * lax.rsqrt(_head_meansq(q, seg) + EPS) * qg_ref[...]
    q_ref[0] = q.astype(BF16)
    k = _dot(h, w_ref[:, D_MODEL:2 * D_MODEL])
    k = k * lax.rsqrt(_head_meansq(k, seg) + EPS) * kg_ref[...]
    for j in range(TOKEN_TILE // ATTN_TILE):
        kt_ref[0, j] = k[j * ATTN_TILE:(j + 1) * ATTN_TILE, :].T.astype(BF16)
    v_ref[0] = _dot(h, w_ref[:, 2 * D_MODEL:3 * D_MODEL]).astype(BF16)


def _qkv_call(x, gain, w_qkv, q_gain, k_gain, seg):
    B, S, D = x.shape
    n_kblk = TOKEN_TILE // ATTN_TILE
    return pl.pallas_call(
        _qkv_kernel,
        grid=(B, S // TOKEN_TILE),
        in_specs=[
            pl.BlockSpec((1, TOKEN_TILE, D), lambda b, i: (b, i, 0)),
            _const_spec((1, D)),
            _const_spec((D, 3 * D)),
            _const_spec((1, D)),
            _const_spec((1, D)),
            _const_spec((MXU_DIM, MXU_DIM)),
        ],
        out_specs=[
            pl.BlockSpec((1, TOKEN_TILE, D), lambda b, i: (b, i, 0)),
            pl.BlockSpec((1, n_kblk, D, ATTN_TILE), lambda b, i: (b, i, 0, 0)),
            pl.BlockSpec((1, TOKEN_TILE, D), lambda b, i: (b, i, 0)),
        ],
        out_shape=[
            jax.ShapeDtypeStruct((B, S, D), BF16),
            jax.ShapeDtypeStruct((B, S // ATTN_TILE, D, ATTN_TILE), BF16),
            jax.ShapeDtypeStruct((B, S, D), BF16),
        ],
        compiler_params=pltpu.CompilerParams(
            dimension_semantics=("arbitrary", "arbitrary"),
            vmem_limit_bytes=VMEM_LIMIT_BYTES),
        name="qkv_proj",
    )(x, gain, w_qkv, q_gain, k_gain, seg)


def _log2_one_minus_sigmoid(nz):
    neg_abs = lax.bitcast_convert_type(
        lax.bitcast_convert_type(nz, jnp.uint32) | jnp.uint32(SIGN_BIT), F32)
    return jnp.minimum(nz, 0.0) - jnp.log2(1.0 + jnp.exp2(neg_abs))


def _block_weights(nq, kt, tri2, carry, mask):
    nz = _dot(nq, kt)
    l2 = _log2_one_minus_sigmoid(nz)
    if mask is not None:
        l2 = jnp.where(mask, l2, 0.0)
    hi, lo = _split_bf16(l2)
    c = _dot(jnp.concatenate([hi, lo], axis=1), tri2) + carry
    a = jnp.exp2(c - nz)
    if mask is not None:
        a = jnp.where(mask, a, 0.0)
    return a.astype(BF16), c[:, 0:1]


def _attn_kernel(q_ref, kt_ref, v_ref, tri_ref, o_ref):
    qi = pl.program_id(2)
    T = ATTN_TILE
    q = q_ref[0]
    first = lax.broadcasted_iota(jnp.int32, (T, LANES), 1) < HEAD_DIM
    zero = jnp.zeros_like(q)
    nq = jnp.concatenate([jnp.where(first, q, zero), jnp.where(first, zero, q)], axis=0)
    tri2 = tri_ref[...]

    row = lax.broadcasted_iota(jnp.int32, (2 * T, T), 0)
    col = lax.broadcasted_iota(jnp.int32, (2 * T, T), 1)
    diag_mask = col < jnp.where(row >= T, row - T, row)

    def visit(kb, acc, carry, mask):
        kt = kt_ref[0, kb]
        v = v_ref[0, pl.ds(pl.multiple_of(kb * T, T), T), :]
        a, carry = _block_weights(nq, kt, tri2, carry, mask)
        return acc + _dot(a, v), carry

    acc, carry = visit(qi, jnp.zeros((2 * T, LANES), F32), jnp.zeros((2 * T, 1), F32),
                       diag_mask)

    def more_to_do(loop_state):
        kb, worst, _, _ = loop_state
        return jnp.logical_and(kb >= 0, worst > LOG2_WEIGHT_FLOOR)

    def step(loop_state):
        kb, _, acc, carry = loop_state
        acc, carry = visit(kb, acc, carry, None)
        return kb - 1, jnp.max(carry), acc, carry

    _, _, acc, _ = lax.while_loop(more_to_do, step, (qi - 1, jnp.max(carry), acc, carry))
    o_ref[0] = jnp.where(first, acc[0:T], acc[T:2 * T])


def _attn_call(q, kt, v, tri):
    B, S, D = q.shape
    T = ATTN_TILE
    return pl.pallas_call(
        _attn_kernel,
        grid=(B, D // LANES, S // T),
        in_specs=[
            pl.BlockSpec((1, T, LANES), lambda b, p, i: (b, i, p)),
            pl.BlockSpec((1, S // T, LANES, T), lambda b, p, i: (b, 0, p, 0)),
            pl.BlockSpec((1, S, LANES), lambda b, p, i: (b, 0, p)),
            _const_spec((2 * T, T)),
        ],
        out_specs=pl.BlockSpec((1, T, LANES), lambda b, p, i: (b, i, p)),
        out_shape=jax.ShapeDtypeStruct((B, S, D), F32),
        compiler_params=pltpu.CompilerParams(
            dimension_semantics=("arbitrary", "arbitrary", "arbitrary"),
            vmem_limit_bytes=VMEM_LIMIT_BYTES),
        name="stickbreak_attn",
    )(q, kt, v, tri)


def _ffn_tile(x1, g_ref, wgu_ref, wd_ref, act_ref):
    h = _rmsnorm_rows(x1, g_ref[...]).astype(BF16)
    for c in range(D_FF // FF_CHUNK):
        lo = c * FF_CHUNK
        gate = _dot(h, wgu_ref[:, lo:lo + FF_CHUNK])
        up = _dot(h, wgu_ref[:, D_FF + lo:D_FF + lo + FF_CHUNK])
        act = gate / (1.0 + jnp.exp(-gate)) * up
        act_ref[:, lo:lo + FF_CHUNK] = act.astype(BF16)
    return x1 + _dot(act_ref[...], wd_ref[...])


def _attn_tail_kernel(x_ref, o_ref, wo_ref, g_ref, wgu_ref, wd_ref, out_ref, act_ref):
    rows_per_block = o_ref.shape[2]
    o = jnp.concatenate([o_ref[0, :, r, :] for r in range(rows_per_block)], axis=0)
    x1 = x_ref[0] + _dot(o.astype(BF16), wo_ref[...])
    out_ref[0] = _ffn_tile(x1, g_ref, wgu_ref, wd_ref, act_ref)


def _attn_tail_call(x, o, w_o, gain, w_gu, w_down):
    B, S, D = x.shape
    n_qblk = S // Q_BLOCK
    assert TOKEN_TILE % n_qblk == 0 and (TOKEN_TILE // n_qblk) % SUBLANES == 0
    row_spec = pl.BlockSpec((1, TOKEN_TILE, D), lambda b, i: (b, i, 0))
    return pl.pallas_call(
        _attn_tail_kernel,
        grid=(B, S // TOKEN_TILE),
        in_specs=[row_spec,
                  pl.BlockSpec((1, n_qblk, TOKEN_TILE // n_qblk, D), lambda b, i: (b, 0, i, 0)),
                  _const_spec((D, D)), _const_spec((1, D)),
                  _const_spec((D, 2 * D_FF)), _const_spec((D_FF, D))],
        out_specs=row_spec,
        out_shape=jax.ShapeDtypeStruct((B, S, D), F32),
        scratch_shapes=[pltpu.VMEM((TOKEN_TILE, D_FF), BF16)],
        compiler_params=pltpu.CompilerParams(
            dimension_semantics=("arbitrary", "arbitrary"),
            vmem_limit_bytes=VMEM_LIMIT_BYTES),
        name="outproj_ffn",
    )(x, o.reshape(B, n_qblk, Q_BLOCK, D), w_o, gain, w_gu, w_down)


def _pool_tail_kernel(x_ref, halo_ref, mg_ref, pw_ref, ps_ref, g_ref, wgu_ref, wd_ref,
                      out_ref, hist_ref, act_ref):
    si = pl.program_id(1)
    T = TOKEN_TILE
    x = x_ref[0]
    h = _rmsnorm_rows(x, mg_ref[...])
    h_halo = _rmsnorm_rows(halo_ref[0], mg_ref[...])
    hist_ref[0:POOL_HALO, :] = jnp.where(si > 0, h_halo, 0.0)
    hist_ref[POOL_HALO:POOL_HALO + T, :] = h

    pos = si * T + lax.broadcasted_iota(jnp.int32, (T, 1), 0)
    mixed = []
    for g, w in enumerate(POOL_WINDOWS):
        lo = g * POOL_GROUP_DIM
        acc = h[:, lo:lo + POOL_GROUP_DIM]
        for back in range(1, w):
            acc = acc + hist_ref[POOL_HALO - back:POOL_HALO - back + T, lo:lo + POOL_GROUP_DIM]
        count = jnp.minimum(pos + 1, w).astype(F32)
        pooled = acc / count - h[:, lo:lo + POOL_GROUP_DIM]
        mixed.append(_dot(pooled.astype(BF16), pw_ref[g]))
    x1 = x + jnp.concatenate(mixed, axis=-1) * ps_ref[...]
    out_ref[0] = _ffn_tile(x1, g_ref, wgu_ref, wd_ref, act_ref)


def _pool_tail_call(x, mix_gain, pool_w, pool_scale, gain, w_gu, w_down):
    B, S, D = x.shape
    T = TOKEN_TILE
    halo_blocks = T // POOL_HALO
    return pl.pallas_call(
        _pool_tail_kernel,
        grid=(B, S // T),
        in_specs=[
            pl.BlockSpec((1, T, D), lambda b, i: (b, i, 0)),
            pl.BlockSpec((1, POOL_HALO, D),
                         lambda b, i: (b, jnp.maximum(i * halo_blocks - 1, 0), 0)),
            _const_spec((1, D)),
            _const_spec(pool_w.shape),
            _const_spec((1, D)),
            _const_spec((1, D)),
            _const_spec((D, 2 * D_FF)),
            _const_spec((D_FF, D)),
        ],
        out_specs=pl.BlockSpec((1, T, D), lambda b, i: (b, i, 0)),
        out_shape=jax.ShapeDtypeStruct((B, S, D), F32),
        scratch_shapes=[pltpu.VMEM((POOL_HALO + T, D), F32),
                        pltpu.VMEM((T, D_FF), BF16)],
        compiler_params=pltpu.CompilerParams(
            dimension_semantics=("arbitrary", "arbitrary"),
            vmem_limit_bytes=VMEM_LIMIT_BYTES),
        name="pool_ffn",
    )(x, x, mix_gain, pool_w, pool_scale, gain, w_gu, w_down)


def kernel(x, mix_norm, ffn_norm, attn_w_qkv, attn_q_norm, attn_k_norm, attn_w_o,
           pool_w, pool_scale, ffn_w_gu, ffn_w_down):
    B, S, D = x.shape
    depth = mix_norm.shape[0]
    assert D == D_MODEL and S % TOKEN_TILE == 0 and TOKEN_TILE % ATTN_TILE == 0

    idx = jnp.arange(MXU_DIM)
    seg = (idx[:, None] // HEAD_DIM == idx[None, :] // HEAD_DIM).astype(BF16)
    kidx = jnp.arange(ATTN_TILE)
    tri = (kidx[:, None] >= kidx[None, :]).astype(BF16)
    tri = jnp.concatenate([tri, tri], axis=0)

    for i in range(depth):
        j = i // 2
        gain = ffn_norm[i][None, :]
        w_gu = ffn_w_gu[i].astype(BF16)
        w_down = ffn_w_down[i].astype(BF16)
        if i % 2 == 0:
            q_gain = jnp.tile(attn_q_norm[j], N_HEADS)[None, :] * (-LOG2_E * HEAD_DIM ** -0.5)
            k_gain = jnp.tile(attn_k_norm[j], N_HEADS)[None, :]
            q, kt, v = _qkv_call(x, mix_norm[i][None, :], attn_w_qkv[j].astype(BF16),
                                 q_gain, k_gain, seg)
            o = _attn_call(q, kt, v, tri)
            x = _attn_tail_call(x, o, attn_w_o[j].astype(BF16), gain, w_gu, w_down)
        else:
            x = _pool_tail_call(x, mix_norm[i][None, :], pool_w[j].astype(BF16),
                                pool_scale[j][None, :], gain, w_gu, w_down)
    return x
```

```python
import jax
import jax.numpy as jnp
from jax import lax
from jax.experimental import pallas as pl
from jax.experimental.pallas import tpu as pltpu

D_MODEL = 1024
N_HEADS = 16
HEAD_DIM = 64
D_FF = 2816
POOL_WINDOWS = (2, 4, 8, 16)
POOL_GROUP_DIM = D_MODEL // len(POOL_WINDOWS)
POOL_HALO = 16
EPS = 1e-6

Q_BLOCK = 128
LANES = 128
SUBLANES = 8
MXU_DIM = 256
VMEM_LIMIT_BYTES = 56 * 1024 * 1024

TOKEN_TILE = 512
ATTN_TILE = 256
LOG2_WEIGHT_FLOOR = -152.0
NO_BLOCK_BIAS = -1.0e4
LOG2_E = 1.4426950408889634
FF_CHUNK = 704

BF16 = jnp.bfloat16
F32 = jnp.float32


def _dot(a, b):
    return jnp.dot(a, b, preferred_element_type=F32)


def _rmsnorm_rows(x, gain):
    ms = jnp.mean(x * x, axis=-1, keepdims=True)
    return x * lax.rsqrt(ms + EPS) * gain


def _split_bf16(x):
    hi = x.astype(BF16)
    lo = (x - hi.astype(F32)).astype(BF16)
    return hi, lo


def _const_spec(shape):
    zeros = (0,) * len(shape)
    return pl.BlockSpec(shape, lambda *_: zeros)


def _head_meansq(t, seg):
    sq = t * t
    outs = []
    for c in range(D_MODEL // MXU_DIM):
        hi, lo = _split_bf16(sq[:, c * MXU_DIM:(c + 1) * MXU_DIM])
        outs.append(_dot(hi, seg) + _dot(lo, seg))
    return jnp.concatenate(outs, axis=-1) * (1.0 / HEAD_DIM)


def _qkv_kernel(x_ref, g_ref, w_ref, qg_ref, kg_ref, seg_ref, q_ref, kt_ref, v_ref):
    h = _rmsnorm_rows(x_ref[0], g_ref[...]).astype(BF16)
    seg = seg_ref[...]
    q = _dot(h, w_ref[:, 0:D_MODEL])
    q = q * lax.rsqrt(_head_meansq(q, seg) + EPS) * qg_ref[...]
    q_ref[0] = q.astype(BF16)
    k = _dot(h, w_ref[:, D_MODEL:2 * D_MODEL])
    k = k * lax.rsqrt(_head_meansq(k, seg) + EPS) * kg_ref[...]
    for j in range(TOKEN_TILE // ATTN_TILE):
        kt_ref[0, j] = k[j * ATTN_TILE:(j + 1) * ATTN_TILE, :].T.astype(BF16)
    v_ref[0] = _dot(h, w_ref[:, 2 * D_MODEL:3 * D_MODEL]).astype(BF16)


def _qkv_call(x, gain, w_qkv, q_gain, k_gain, seg):
    B, S, D = x.shape
    n_kblk = TOKEN_TILE // ATTN_TILE
    return pl.pallas_call(
        _qkv_kernel,
        grid=(B, S // TOKEN_TILE),
        in_specs=[
            pl.BlockSpec((1, TOKEN_TILE, D), lambda b, i: (b, i, 0)),
            _const_spec((1, D)),
            _const_spec((D, 3 * D)),
            _const_spec((1, D)),
            _const_spec((1, D)),
            _const_spec((MXU_DIM, MXU_DIM)),
        ],
        out_specs=[
            pl.BlockSpec((1, TOKEN_TILE, D), lambda b, i: (b, i, 0)),
            pl.BlockSpec((1, n_kblk, D, ATTN_TILE), lambda b, i: (b, i, 0, 0)),
            pl.BlockSpec((1, TOKEN_TILE, D), lambda b, i: (b, i, 0)),
        ],
        out_shape=[
            jax.ShapeDtypeStruct((B, S, D), BF16),
            jax.ShapeDtypeStruct((B, S // ATTN_TILE, D, ATTN_TILE), BF16),
            jax.ShapeDtypeStruct((B, S, D), BF16),
        ],
        compiler_params=pltpu.CompilerParams(
            dimension_semantics=("arbitrary", "arbitrary"),
            vmem_limit_bytes=VMEM_LIMIT_BYTES),
        name="qkv_proj",
    )(x, gain, w_qkv, q_gain, k_gain, seg)


def _log2_one_minus_sigmoid(nz):
    return jnp.minimum(nz, 0.0) - jnp.log2(1.0 + jnp.exp2(-jnp.abs(nz)))


def _block_weights(nq, kt, tri2, carry, mask):
    nz = _dot(nq, kt)
    l2 = _log2_one_minus_sigmoid(nz)
    if mask is not None:
        l2 = jnp.where(mask, l2, 0.0)
    hi, lo = _split_bf16(l2)
    c = _dot(jnp.concatenate([hi, lo], axis=1), tri2) + carry
    a = jnp.exp2(c - nz)
    if mask is not None:
        a = jnp.where(mask, a, 0.0)
    return a.astype(BF16), c[:, 0:1]


def _stacked_tri(width):
    idx = jnp.arange(width)
    tri = (idx[:, None] >= idx[None, :]).astype(BF16)
    return jnp.concatenate([tri, tri], axis=0)


def _attn_kernel(q_ref, kt_ref, v_ref, tri_ref, o_ref):
    qi = pl.program_id(2)
    T = ATTN_TILE
    q = q_ref[0]
    first = lax.broadcasted_iota(jnp.int32, (T, LANES), 1) < HEAD_DIM
    zero = jnp.zeros_like(q)
    nq = jnp.concatenate([jnp.where(first, q, zero), jnp.where(first, zero, q)], axis=0)

    def key_block(kb):
        return kt_ref[0, kb], v_ref[0, pl.ds(pl.multiple_of(kb * T, T), T), :]

    prev = jnp.maximum(qi - 1, 0)
    kt_prev, v_prev = key_block(prev)
    kt_diag, v_diag = key_block(qi)
    row = lax.broadcasted_iota(jnp.int32, (2 * T, 1), 0)
    col = lax.broadcasted_iota(jnp.int32, (1, T), 1)
    mask = col < jnp.where(row >= T, row - T, row)
    nz = _dot(nq, jnp.concatenate([kt_prev, kt_diag], axis=1))
    nz_prev, nz_diag = nz[:, 0:T], nz[:, T:2 * T]
    l2_prev = _log2_one_minus_sigmoid(nz_prev)
    l2_diag = jnp.where(mask, _log2_one_minus_sigmoid(nz_diag), 0.0)
    hi, lo = _split_bf16(jnp.concatenate([l2_prev, l2_diag], axis=0))
    c = _dot(jnp.concatenate([hi, lo], axis=1), tri_ref[...])
    c_diag = c[2 * T:4 * T]
    c_prev = c[0:2 * T] + (c_diag[:, 0:1] + jnp.where(qi > 0, 0.0, NO_BLOCK_BIAS))
    carry = c_prev[:, 0:1]
    a_prev = jnp.exp2(c_prev - nz_prev)
    a_diag = jnp.where(mask, jnp.exp2(c_diag - nz_diag), 0.0)
    a = jnp.concatenate([a_prev, a_diag], axis=1).astype(BF16)
    acc = _dot(a, jnp.concatenate([v_prev, v_diag], axis=0))

    def more_to_do(loop_state):
        kb, worst, _, _ = loop_state
        return jnp.logical_and(kb >= 0, worst > LOG2_WEIGHT_FLOOR)

    def step(loop_state):
        kb, _, acc, carry = loop_state
        kt, v = key_block(kb)
        a, carry = _block_weights(nq, kt, tri_ref[...], carry, None)
        return kb - 1, jnp.max(carry), acc + _dot(a, v), carry

    _, _, acc, _ = lax.while_loop(more_to_do, step, (qi - 2, jnp.max(carry), acc, carry))
    o_ref[0] = jnp.where(first, acc[0:T], acc[T:2 * T])


def _attn_call(q, kt, v, tri):
    B, S, D = q.shape
    T = ATTN_TILE
    return pl.pallas_call(
        _attn_kernel,
        grid=(B, D // LANES, S // T),
        in_specs=[
            pl.BlockSpec((1, T, LANES), lambda b, p, i: (b, i, p)),
            pl.BlockSpec((1, S // T, LANES, T), lambda b, p, i: (b, 0, p, 0)),
            pl.BlockSpec((1, S, LANES), lambda b, p, i: (b, 0, p)),
            _const_spec((2 * T, T)),
        ],
        out_specs=pl.BlockSpec((1, T, LANES), lambda b, p, i: (b, i, p)),
        out_shape=jax.ShapeDtypeStruct((B, S, D), F32),
        compiler_params=pltpu.CompilerParams(
            dimension_semantics=("arbitrary", "arbitrary", "arbitrary"),
            vmem_limit_bytes=VMEM_LIMIT_BYTES),
        name="stickbreak_attn",
    )(q, kt, v, tri)


def _ffn_tile(x1, g_ref, wgu_ref, wd_ref, act_ref):
    h = _rmsnorm_rows(x1, g_ref[...]).astype(BF16)
    for c in range(D_FF // FF_CHUNK):
        lo = c * FF_CHUNK
        gate = _dot(h, wgu_ref[:, lo:lo + FF_CHUNK])
        up = _dot(h, wgu_ref[:, D_FF + lo:D_FF + lo + FF_CHUNK])
        act = gate / (1.0 + jnp.exp(-gate)) * up
        act_ref[:, lo:lo + FF_CHUNK] = act.astype(BF16)
    return x1 + _dot(act_ref[...], wd_ref[...])


def _attn_tail_kernel(x_ref, o_ref, wo_ref, g_ref, wgu_ref, wd_ref, out_ref, act_ref):
    rows_per_block = o_ref.shape[2]
    o = jnp.concatenate([o_ref[0, :, r, :] for r in range(rows_per_block)], axis=0)
    x1 = x_ref[0] + _dot(o.astype(BF16), wo_ref[...])
    out_ref[0] = _ffn_tile(x1, g_ref, wgu_ref, wd_ref, act_ref)


def _attn_tail_call(x, o, w_o, gain, w_gu, w_down):
    B, S, D = x.shape
    n_qblk = S // Q_BLOCK
    assert TOKEN_TILE % n_qblk == 0 and (TOKEN_TILE // n_qblk) % SUBLANES == 0
    row_spec = pl.BlockSpec((1, TOKEN_TILE, D), lambda b, i: (b, i, 0))
    return pl.pallas_call(
        _attn_tail_kernel,
        grid=(B, S // TOKEN_TILE),
        in_specs=[row_spec,
                  pl.BlockSpec((1, n_qblk, TOKEN_TILE // n_qblk, D), lambda b, i: (b, 0, i, 0)),
                  _const_spec((D, D)), _const_spec((1, D)),
                  _const_spec((D, 2 * D_FF)), _const_spec((D_FF, D))],
        out_specs=row_spec,
        out_shape=jax.ShapeDtypeStruct((B, S, D), F32),
        scratch_shapes=[pltpu.VMEM((TOKEN_TILE, D_FF), BF16)],
        compiler_params=pltpu.CompilerParams(
            dimension_semantics=("arbitrary", "arbitrary"),
            vmem_limit_bytes=VMEM_LIMIT_BYTES),
        name="outproj_ffn",
    )(x, o.reshape(B, n_qblk, Q_BLOCK, D), w_o, gain, w_gu, w_down)


def _pool_tail_kernel(x_ref, halo_ref, mg_ref, pw_ref, ps_ref, g_ref, wgu_ref, wd_ref,
                      out_ref, hist_ref, act_ref):
    si = pl.program_id(1)
    T = TOKEN_TILE
    x = x_ref[0]
    h = _rmsnorm_rows(x, mg_ref[...])
    h_halo = _rmsnorm_rows(halo_ref[0], mg_ref[...])
    hist_ref[0:POOL_HALO, :] = jnp.where(si > 0, h_halo, 0.0)
    hist_ref[POOL_HALO:POOL_HALO + T, :] = h

    pos = si * T + lax.broadcasted_iota(jnp.int32, (T, 1), 0)
    mixed = []
    for g, w in enumerate(POOL_WINDOWS):
        lo = g * POOL_GROUP_DIM
        acc = h[:, lo:lo + POOL_GROUP_DIM]
        for back in range(1, w):
            acc = acc + hist_ref[POOL_HALO - back:POOL_HALO - back + T, lo:lo + POOL_GROUP_DIM]
        count = jnp.minimum(pos + 1, w).astype(F32)
        pooled = acc / count - h[:, lo:lo + POOL_GROUP_DIM]
        mixed.append(_dot(pooled.astype(BF16), pw_ref[g]))
    x1 = x + jnp.concatenate(mixed, axis=-1) * ps_ref[...]
    out_ref[0] = _ffn_tile(x1, g_ref, wgu_ref, wd_ref, act_ref)


def _pool_tail_call(x, mix_gain, pool_w, pool_scale, gain, w_gu, w_down):
    B, S, D = x.shape
    T = TOKEN_TILE
    halo_blocks = T // POOL_HALO
    return pl.pallas_call(
        _pool_tail_kernel,
        grid=(B, S // T),
        in_specs=[
            pl.BlockSpec((1, T, D), lambda b, i: (b, i, 0)),
            pl.BlockSpec((1, POOL_HALO, D),
                         lambda b, i: (b, jnp.maximum(i * halo_blocks - 1, 0), 0)),
            _const_spec((1, D)),
            _const_spec(pool_w.shape),
            _const_spec((1, D)),
            _const_spec((1, D)),
            _const_spec((D, 2 * D_FF)),
            _const_spec((D_FF, D)),
        ],
        out_specs=pl.BlockSpec((1, T, D), lambda b, i: (b, i, 0)),
        out_shape=jax.ShapeDtypeStruct((B, S, D), F32),
        scratch_shapes=[pltpu.VMEM((POOL_HALO + T, D), F32),
                        pltpu.VMEM((T, D_FF), BF16)],
        compiler_params=pltpu.CompilerParams(
            dimension_semantics=("arbitrary", "arbitrary"),
            vmem_limit_bytes=VMEM_LIMIT_BYTES),
        name="pool_ffn",
    )(x, x, mix_gain, pool_w, pool_scale, gain, w_gu, w_down)


def kernel(x, mix_norm, ffn_norm, attn_w_qkv, attn_q_norm, attn_k_norm, attn_w_o,
           pool_w, pool_scale, ffn_w_gu, ffn_w_down):
    B, S, D = x.shape
    depth = mix_norm.shape[0]
    assert D == D_MODEL and S % TOKEN_TILE == 0 and TOKEN_TILE % ATTN_TILE == 0

    idx = jnp.arange(MXU_DIM)
    seg = (idx[:, None] // HEAD_DIM == idx[None, :] // HEAD_DIM).astype(BF16)
    tri = _stacked_tri(ATTN_TILE)

    for i in range(depth):
        j = i // 2
        gain = ffn_norm[i][None, :]
        w_gu = ffn_w_gu[i].astype(BF16)
        w_down = ffn_w_down[i].astype(BF16)
        if i % 2 == 0:
            q_gain = jnp.tile(attn_q_norm[j], N_HEADS)[None, :] * (-LOG2_E * HEAD_DIM ** -0.5)
            k_gain = jnp.tile(attn_k_norm[j], N_HEADS)[None, :]
            q, kt, v = _qkv_call(x, mix_norm[i][None, :], attn_w_qkv[j].astype(BF16),
                                 q_gain, k_gain, seg)
            o = _attn_call(q, kt, v, tri)
            x = _attn_tail_call(x, o, attn_w_o[j].astype(BF16), gain, w_gu, w_down)
        else:
            x = _pool_tail_call(x, mix_norm[i][None, :], pool_w[j].astype(BF16),
                                pool_scale[j][None, :], gain, w_gu, w_down)
    return x
```

```python
import jax
import jax.numpy as jnp
from jax import lax
from jax.experimental import pallas as pl
from jax.experimental.pallas import tpu as pltpu

D_MODEL = 1024
N_HEADS = 16
HEAD_DIM = 64
D_FF = 2816
POOL_WINDOWS = (2, 4, 8, 16)
POOL_GROUP_DIM = D_MODEL // len(POOL_WINDOWS)
POOL_HALO = 16
EPS = 1e-6

Q_BLOCK = 128
LANES = 128
SUBLANES = 8
MXU_DIM = 256
VMEM_LIMIT_BYTES = 56 * 1024 * 1024

TOKEN_TILE = 512
ATTN_TILE = 256
ATTN_HEADS = 2
ATTN_PREV = 1
LOG2_WEIGHT_FLOOR = -152.0
NO_BLOCK_BIAS = -1.0e4
LOG2_E = 1.4426950408889634
FF_CHUNK = MXU_DIM

BF16 = jnp.bfloat16
F32 = jnp.float32


def _dot(a, b):
    return jnp.dot(a, b, preferred_element_type=F32)


def _rmsnorm_rows(x, gain):
    ms = jnp.mean(x * x, axis=-1, keepdims=True)
    return x * lax.rsqrt(ms + EPS) * gain


def _split_bf16(x):
    hi = x.astype(BF16)
    lo = (x - hi.astype(F32)).astype(BF16)
    return hi, lo


def _const_spec(shape):
    zeros = (0,) * len(shape)
    return pl.BlockSpec(shape, lambda *_: zeros)


def _head_meansq(t, seg):
    sq = (t * t).astype(BF16)
    outs = [_dot(sq[:, c * MXU_DIM:(c + 1) * MXU_DIM], seg) for c in range(D_MODEL // MXU_DIM)]
    return jnp.concatenate(outs, axis=-1) * (1.0 / HEAD_DIM)


def _qkv_kernel(x_ref, g_ref, w_ref, qg_ref, kg_ref, seg_ref, q_ref, kt_ref, v_ref):
    h = _rmsnorm_rows(x_ref[0], g_ref[...]).astype(BF16)
    seg = seg_ref[...]
    q = _dot(h, w_ref[:, 0:D_MODEL])
    q = q * lax.rsqrt(_head_meansq(q, seg) + EPS) * qg_ref[...]
    q_ref[0] = q.astype(BF16)
    k = _dot(h, w_ref[:, D_MODEL:2 * D_MODEL])
    k = k * lax.rsqrt(_head_meansq(k, seg) + EPS) * kg_ref[...]
    for j in range(TOKEN_TILE // ATTN_TILE):
        kt_ref[0, j] = k[j * ATTN_TILE:(j + 1) * ATTN_TILE, :].T.astype(BF16)
    v_ref[0] = _dot(h, w_ref[:, 2 * D_MODEL:3 * D_MODEL]).astype(BF16)


def _qkv_call(x, gain, w_qkv, q_gain, k_gain, seg):
    B, S, D = x.shape
    n_kblk = TOKEN_TILE // ATTN_TILE
    return pl.pallas_call(
        _qkv_kernel,
        grid=(B, S // TOKEN_TILE),
        in_specs=[
            pl.BlockSpec((1, TOKEN_TILE, D), lambda b, i: (b, i, 0)),
            _const_spec((1, D)),
            _const_spec((D, 3 * D)),
            _const_spec((1, D)),
            _const_spec((1, D)),
            _const_spec((MXU_DIM, MXU_DIM)),
        ],
        out_specs=[
            pl.BlockSpec((1, TOKEN_TILE, D), lambda b, i: (b, i, 0)),
            pl.BlockSpec((1, n_kblk, D, ATTN_TILE), lambda b, i: (b, i, 0, 0)),
            pl.BlockSpec((1, TOKEN_TILE, D), lambda b, i: (b, i, 0)),
        ],
        out_shape=[
            jax.ShapeDtypeStruct((B, S, D), BF16),
            jax.ShapeDtypeStruct((B, S // ATTN_TILE, D, ATTN_TILE), BF16),
            jax.ShapeDtypeStruct((B, S, D), BF16),
        ],
        compiler_params=pltpu.CompilerParams(
            dimension_semantics=("arbitrary", "arbitrary"),
            vmem_limit_bytes=VMEM_LIMIT_BYTES),
        name="qkv_proj",
    )(x, gain, w_qkv, q_gain, k_gain, seg)


def _log2_one_minus_sigmoid(nz):
    return jnp.minimum(nz, 0.0) - jnp.log2(1.0 + jnp.exp2(-jnp.abs(nz)))


def _block_weights(nq, kt, tri2, carry, mask):
    nz = _dot(nq, kt)
    l2 = _log2_one_minus_sigmoid(nz)
    if mask is not None:
        l2 = jnp.where(mask, l2, 0.0)
    hi, lo = _split_bf16(l2)
    c = _dot(jnp.concatenate([hi, lo], axis=1), tri2) + carry
    a = jnp.exp2(c - nz)
    if mask is not None:
        a = jnp.where(mask, a, 0.0)
    return a.astype(BF16), c[:, 0:1]


def _stacked_tri(width):
    idx = jnp.arange(width)
    tri = (idx[:, None] >= idx[None, :]).astype(BF16)
    return jnp.concatenate([tri, tri], axis=0)


def _attn_kernel(q_ref, kt_ref, v_ref, tri_ref, o_ref):
    qi = pl.program_id(2)
    T, G = ATTN_TILE, ATTN_HEADS
    M = G * T
    q = q_ref[0]
    head_of_lane = lax.broadcasted_iota(jnp.int32, (T, G * HEAD_DIM), 1) // HEAD_DIM
    zero = jnp.zeros_like(q)
    nq = jnp.concatenate([jnp.where(head_of_lane == h, q, zero) for h in range(G)], axis=0)

    def key_block(kb):
        return kt_ref[0, kb], v_ref[0, pl.ds(pl.multiple_of(kb * T, T), T), :]

    P = ATTN_PREV
    blocks = [key_block(jnp.maximum(qi - d, 0)) for d in range(P, -1, -1)]
    row = lax.broadcasted_iota(jnp.int32, (M, 1), 0)
    col = lax.broadcasted_iota(jnp.int32, (1, T), 1)
    mask = col < (row & (T - 1))
    nz = _dot(nq, jnp.concatenate([kt for kt, _ in blocks], axis=1))
    nz = [nz[:, j * T:(j + 1) * T] for j in range(P + 1)]
    l2 = [_log2_one_minus_sigmoid(nz[j]) for j in range(P)]
    l2.append(jnp.where(mask, _log2_one_minus_sigmoid(nz[P]), 0.0))
    hi, lo = _split_bf16(jnp.concatenate(l2, axis=0))
    c = _dot(jnp.concatenate([hi, lo], axis=1), tri_ref[...])
    c = [c[j * M:(j + 1) * M] for j in range(P + 1)]
    a = [None] * P + [jnp.where(mask, jnp.exp2(c[P] - nz[P]), 0.0)]
    for j in range(P - 1, -1, -1):
        exists = qi >= P - j
        c[j] = c[j] + (c[j + 1][:, 0:1] + jnp.where(exists, 0.0, NO_BLOCK_BIAS))
        a[j] = jnp.exp2(c[j] - nz[j])
    carry = c[0][:, 0:1]
    acc = _dot(jnp.concatenate(a, axis=1).astype(BF16),
               jnp.concatenate([v for _, v in blocks], axis=0))

    def more_to_do(loop_state):
        kb, worst, _, _ = loop_state
        return jnp.logical_and(kb >= 0, worst > LOG2_WEIGHT_FLOOR)

    def step(loop_state):
        kb, _, acc, carry = loop_state
        kt, v = key_block(kb)
        a, carry = _block_weights(nq, kt, tri_ref[...], carry, None)
        return kb - 1, jnp.max(carry), acc + _dot(a, v), carry

    _, _, acc, _ = lax.while_loop(more_to_do, step, (qi - 1 - P, jnp.max(carry), acc, carry))
    out = acc[0:T]
    for h in range(1, G):
        out = jnp.where(head_of_lane == h, acc[h * T:(h + 1) * T], out)
    o_ref[0] = out


def _attn_call(q, kt, v, tri):
    B, S, D = q.shape
    T, W = ATTN_TILE, ATTN_HEADS * HEAD_DIM
    return pl.pallas_call(
        _attn_kernel,
        grid=(B, D // W, S // T),
        in_specs=[
            pl.BlockSpec((1, T, W), lambda b, p, i: (b, i, p)),
            pl.BlockSpec((1, S // T, W, T), lambda b, p, i: (b, 0, p, 0)),
            pl.BlockSpec((1, S, W), lambda b, p, i: (b, 0, p)),
            _const_spec((2 * T, T)),
        ],
        out_specs=pl.BlockSpec((1, T, W), lambda b, p, i: (b, i, p)),
        out_shape=jax.ShapeDtypeStruct((B, S, D), F32),
        compiler_params=pltpu.CompilerParams(
            dimension_semantics=("arbitrary", "arbitrary", "arbitrary"),
            vmem_limit_bytes=VMEM_LIMIT_BYTES),
        name="stickbreak_attn",
    )(q, kt, v, tri)


def _ffn_tile(x1, g_ref, wgu_ref, wd_ref, act_ref):
    h = _rmsnorm_rows(x1, g_ref[...]).astype(BF16)
    for c in range(D_FF // FF_CHUNK):
        lo = c * FF_CHUNK
        gate = _dot(h, wgu_ref[:, lo:lo + FF_CHUNK])
        up = _dot(h, wgu_ref[:, D_FF + lo:D_FF + lo + FF_CHUNK])
        act = gate / (1.0 + jnp.exp(-gate)) * up
        act_ref[:, lo:lo + FF_CHUNK] = act.astype(BF16)
    return x1 + _dot(act_ref[...], wd_ref[...])


def _attn_tail_kernel(x_ref, o_ref, wo_ref, g_ref, wgu_ref, wd_ref, out_ref, act_ref):
    rows_per_block = o_ref.shape[2]
    o = jnp.concatenate([o_ref[0, :, r, :] for r in range(rows_per_block)], axis=0)
    x1 = x_ref[0] + _dot(o.astype(BF16), wo_ref[...])
    out_ref[0] = _ffn_tile(x1, g_ref, wgu_ref, wd_ref, act_ref)


def _attn_tail_call(x, o, w_o, gain, w_gu, w_down):
    B, S, D = x.shape
    n_qblk = S // Q_BLOCK
    assert TOKEN_TILE % n_qblk == 0 and (TOKEN_TILE // n_qblk) % SUBLANES == 0
    row_spec = pl.BlockSpec((1, TOKEN_TILE, D), lambda b, i: (b, i, 0))
    return pl.pallas_call(
        _attn_tail_kernel,
        grid=(B, S // TOKEN_TILE),
        in_specs=[row_spec,
                  pl.BlockSpec((1, n_qblk, TOKEN_TILE // n_qblk, D), lambda b, i: (b, 0, i, 0)),
                  _const_spec((D, D)), _const_spec((1, D)),
                  _const_spec((D, 2 * D_FF)), _const_spec((D_FF, D))],
        out_specs=row_spec,
        out_shape=jax.ShapeDtypeStruct((B, S, D), F32),
        scratch_shapes=[pltpu.VMEM((TOKEN_TILE, D_FF), BF16)],
        compiler_params=pltpu.CompilerParams(
            dimension_semantics=("arbitrary", "arbitrary"),
            vmem_limit_bytes=VMEM_LIMIT_BYTES),
        name="outproj_ffn",
    )(x, o.reshape(B, n_qblk, Q_BLOCK, D), w_o, gain, w_gu, w_down)


def _pool_tail_kernel(x_ref, halo_ref, mg_ref, pw_ref, ps_ref, g_ref, wgu_ref, wd_ref,
                      out_ref, hist_ref, act_ref):
    si = pl.program_id(1)
    T = TOKEN_TILE
    x = x_ref[0]
    h = _rmsnorm_rows(x, mg_ref[...])
    h_halo = _rmsnorm_rows(halo_ref[0], mg_ref[...])
    hist_ref[0:POOL_HALO, :] = jnp.where(si > 0, h_halo, 0.0)
    hist_ref[POOL_HALO:POOL_HALO + T, :] = h

    pos = si * T + lax.broadcasted_iota(jnp.int32, (T, 1), 0)
    mixed = []
    for g, w in enumerate(POOL_WINDOWS):
        lo = g * POOL_GROUP_DIM
        acc = h[:, lo:lo + POOL_GROUP_DIM]
        for back in range(1, w):
            acc = acc + hist_ref[POOL_HALO - back:POOL_HALO - back + T, lo:lo + POOL_GROUP_DIM]
        count = jnp.minimum(pos + 1, w).astype(F32)
        pooled = acc / count - h[:, lo:lo + POOL_GROUP_DIM]
        mixed.append(_dot(pooled.astype(BF16), pw_ref[g]))
    x1 = x + jnp.concatenate(mixed, axis=-1) * ps_ref[...]
    out_ref[0] = _ffn_tile(x1, g_ref, wgu_ref, wd_ref, act_ref)


def _pool_tail_call(x, mix_gain, pool_w, pool_scale, gain, w_gu, w_down):
    B, S, D = x.shape
    T = TOKEN_TILE
    halo_blocks = T // POOL_HALO
    return pl.pallas_call(
        _pool_tail_kernel,
        grid=(B, S // T),
        in_specs=[
            pl.BlockSpec((1, T, D), lambda b, i: (b, i, 0)),
            pl.BlockSpec((1, POOL_HALO, D),
                         lambda b, i: (b, jnp.maximum(i * halo_blocks - 1, 0), 0)),
            _const_spec((1, D)),
            _const_spec(pool_w.shape),
            _const_spec((1, D)),
            _const_spec((1, D)),
            _const_spec((D, 2 * D_FF)),
            _const_spec((D_FF, D)),
        ],
        out_specs=pl.BlockSpec((1, T, D), lambda b, i: (b, i, 0)),
        out_shape=jax.ShapeDtypeStruct((B, S, D), F32),
        scratch_shapes=[pltpu.VMEM((POOL_HALO + T, D), F32),
                        pltpu.VMEM((T, D_FF), BF16)],
        compiler_params=pltpu.CompilerParams(
            dimension_semantics=("arbitrary", "arbitrary"),
            vmem_limit_bytes=VMEM_LIMIT_BYTES),
        name="pool_ffn",
    )(x, x, mix_gain, pool_w, pool_scale, gain, w_gu, w_down)


def kernel(x, mix_norm, ffn_norm, attn_w_qkv, attn_q_norm, attn_k_norm, attn_w_o,
           pool_w, pool_scale, ffn_w_gu, ffn_w_down):
    B, S, D = x.shape
    depth = mix_norm.shape[0]
    assert D == D_MODEL and S % TOKEN_TILE == 0 and TOKEN_TILE % ATTN_TILE == 0

    idx = jnp.arange(MXU_DIM)
    seg = (idx[:, None] // HEAD_DIM == idx[None, :] // HEAD_DIM).astype(BF16)
    tri = _stacked_tri(ATTN_TILE)

    for i in range(depth):
        j = i // 2
        gain = ffn_norm[i][None, :]
        w_gu = ffn_w_gu[i].astype(BF16)
        w_down = ffn_w_down[i].astype(BF16)
        if i % 2 == 0:
            q_gain = jnp.tile(attn_q_norm[j], N_HEADS)[None, :] * (-LOG2_E * HEAD_DIM ** -0.5)
            k_gain = jnp.tile(attn_k_norm[j], N_HEADS)[None, :]
            q, kt, v = _qkv_call(x, mix_norm[i][None, :], attn_w_qkv[j].astype(BF16),
                                 q_gain, k_gain, seg)
            o = _attn_call(q, kt, v, tri)
            x = _attn_tail_call(x, o, attn_w_o[j].astype(BF16), gain, w_gu, w_down)
        else:
            x = _pool_tail_call(x, mix_norm[i][None, :], pool_w[j].astype(BF16),
                                pool_scale[j][None, :], gain, w_gu, w_down)
    return x
```

```python
import jax
import jax.numpy as jnp
from jax import lax
from jax.experimental import pallas as pl
from jax.experimental.pallas import tpu as pltpu

D_MODEL = 1024
N_HEADS = 16
HEAD_DIM = 64
D_FF = 2816
POOL_WINDOWS = (2, 4, 8, 16)
POOL_GROUP_DIM = D_MODEL // len(POOL_WINDOWS)
POOL_HALO = 16
EPS = 1e-6

Q_BLOCK = 128
LANES = 128
SUBLANES = 8
MXU_DIM = 256
VMEM_LIMIT_BYTES = 56 * 1024 * 1024

TOKEN_TILE = 512
ATTN_TILE = 256
ATTN_HEADS = 2
ATTN_CHAINS = 4
ATTN_PREV = 1
LOG2_WEIGHT_FLOOR = -152.0
NO_BLOCK_BIAS = -1.0e4
LOG2_E = 1.4426950408889634
FF_CHUNK = MXU_DIM

BF16 = jnp.bfloat16
F32 = jnp.float32


def _dot(a, b):
    return jnp.dot(a, b, preferred_element_type=F32)


def _rmsnorm_rows(x, gain):
    ms = jnp.mean(x * x, axis=-1, keepdims=True)
    return x * lax.rsqrt(ms + EPS) * gain


def _split_bf16(x):
    hi = x.astype(BF16)
    lo = (x - hi.astype(F32)).astype(BF16)
    return hi, lo


def _const_spec(shape):
    zeros = (0,) * len(shape)
    return pl.BlockSpec(shape, lambda *_: zeros)


def _head_meansq(t, seg):
    sq = (t * t).astype(BF16)
    outs = [_dot(sq[:, c * MXU_DIM:(c + 1) * MXU_DIM], seg) for c in range(D_MODEL // MXU_DIM)]
    return jnp.concatenate(outs, axis=-1) * (1.0 / HEAD_DIM)


def _qkv_kernel(x_ref, g_ref, w_ref, qg_ref, kg_ref, seg_ref, q_ref, kt_ref, v_ref):
    h = _rmsnorm_rows(x_ref[0], g_ref[...]).astype(BF16)
    seg = seg_ref[...]
    q = _dot(h, w_ref[:, 0:D_MODEL])
    q = q * lax.rsqrt(_head_meansq(q, seg) + EPS) * qg_ref[...]
    q_ref[0] = q.astype(BF16)
    k = _dot(h, w_ref[:, D_MODEL:2 * D_MODEL])
    k = k * lax.rsqrt(_head_meansq(k, seg) + EPS) * kg_ref[...]
    for j in range(TOKEN_TILE // ATTN_TILE):
        kt_ref[0, j] = k[j * ATTN_TILE:(j + 1) * ATTN_TILE, :].T.astype(BF16)
    v_ref[0] = _dot(h, w_ref[:, 2 * D_MODEL:3 * D_MODEL]).astype(BF16)


def _qkv_call(x, gain, w_qkv, q_gain, k_gain, seg):
    B, S, D = x.shape
    n_kblk = TOKEN_TILE // ATTN_TILE
    return pl.pallas_call(
        _qkv_kernel,
        grid=(B, S // TOKEN_TILE),
        in_specs=[
            pl.BlockSpec((1, TOKEN_TILE, D), lambda b, i: (b, i, 0)),
            _const_spec((1, D)),
            _const_spec((D, 3 * D)),
            _const_spec((1, D)),
            _const_spec((1, D)),
            _const_spec((MXU_DIM, MXU_DIM)),
        ],
        out_specs=[
            pl.BlockSpec((1, TOKEN_TILE, D), lambda b, i: (b, i, 0)),
            pl.BlockSpec((1, n_kblk, D, ATTN_TILE), lambda b, i: (b, i, 0, 0)),
            pl.BlockSpec((1, TOKEN_TILE, D), lambda b, i: (b, i, 0)),
        ],
        out_shape=[
            jax.ShapeDtypeStruct((B, S, D), BF16),
            jax.ShapeDtypeStruct((B, S // ATTN_TILE, D, ATTN_TILE), BF16),
            jax.ShapeDtypeStruct((B, S, D), BF16),
        ],
        compiler_params=pltpu.CompilerParams(
            dimension_semantics=("arbitrary", "arbitrary"),
            vmem_limit_bytes=VMEM_LIMIT_BYTES),
        name="qkv_proj",
    )(x, gain, w_qkv, q_gain, k_gain, seg)


def _log2_one_minus_sigmoid(nz):
    return jnp.minimum(nz, 0.0) - jnp.log2(1.0 + jnp.exp2(-jnp.abs(nz)))


def _block_weights(nq, kt, tri2, carry, mask):
    nz = _dot(nq, kt)
    l2 = _log2_one_minus_sigmoid(nz)
    if mask is not None:
        l2 = jnp.where(mask, l2, 0.0)
    hi, lo = _split_bf16(l2)
    c = _dot(jnp.concatenate([hi, lo], axis=1), tri2) + carry
    a = jnp.exp2(c - nz)
    if mask is not None:
        a = jnp.where(mask, a, 0.0)
    return a.astype(BF16), c[:, 0:1]


def _stacked_tri(width):
    idx = jnp.arange(width)
    tri = (idx[:, None] >= idx[None, :]).astype(BF16)
    return jnp.concatenate([tri, tri], axis=0)


def _attn_kernel(q_ref, kt_ref, v_ref, tri_ref, o_ref):
    qi = pl.program_id(2)
    T, G, P = ATTN_TILE, ATTN_HEADS, ATTN_PREV
    W = G * HEAD_DIM
    M = G * T
    chains = range(ATTN_CHAINS)
    head_of_lane = lax.broadcasted_iota(jnp.int32, (T, W), 1) // HEAD_DIM
    row = lax.broadcasted_iota(jnp.int32, (M, 1), 0)
    col = lax.broadcasted_iota(jnp.int32, (1, T), 1)
    mask = col < (row & (T - 1))
    tri = tri_ref[...]

    def stacked_q(c):
        q = q_ref[0, :, c * W:(c + 1) * W]
        zero = jnp.zeros_like(q)
        return jnp.concatenate([jnp.where(head_of_lane == h, q, zero) for h in range(G)], axis=0)

    def key_block(c, kb):
        return (kt_ref[0, kb, c * W:(c + 1) * W, :],
                v_ref[0, pl.ds(pl.multiple_of(kb * T, T), T), c * W:(c + 1) * W])

    nq = [stacked_q(c) for c in chains]

    blocks = [[key_block(c, jnp.maximum(qi - d, 0)) for d in range(P, -1, -1)] for c in chains]

    def scores(c):
        nz = _dot(nq[c], jnp.concatenate([kt for kt, _ in blocks[c]], axis=1))
        return [nz[:, j * T:(j + 1) * T] for j in range(P + 1)]

    def log_terms(nz):
        l2 = [_log2_one_minus_sigmoid(nz[j]) for j in range(P)]
        l2.append(jnp.where(mask, _log2_one_minus_sigmoid(nz[P]), 0.0))
        hi, lo = _split_bf16(jnp.concatenate(l2, axis=0))
        return jnp.concatenate([hi, lo], axis=1)

    def weights(nz, c):
        c = [c[j * M:(j + 1) * M] for j in range(P + 1)]
        a = [None] * P + [jnp.where(mask, jnp.exp2(c[P] - nz[P]), 0.0)]
        for j in range(P - 1, -1, -1):
            exists = qi >= P - j
            c[j] = c[j] + (c[j + 1][:, 0:1] + jnp.where(exists, 0.0, NO_BLOCK_BIAS))
            a[j] = jnp.exp2(c[j] - nz[j])
        return jnp.concatenate(a, axis=1).astype(BF16), c[0][:, 0:1]

    nz = [scores(c) for c in chains]
    lhs = [log_terms(nz[c]) for c in chains]
    csum = [_dot(lhs[c], tri) for c in chains]
    a_carry = [weights(nz[c], csum[c]) for c in chains]
    acc = tuple(_dot(a_carry[c][0], jnp.concatenate([v for _, v in blocks[c]], axis=0))
                for c in chains)
    carry = tuple(a_carry[c][1] for c in chains)

    def slowest_decay(carry):
        worst = carry[0]
        for other in carry[1:]:
            worst = jnp.maximum(worst, other)
        return jnp.max(worst)

    def more_to_do(loop_state):
        kb, worst, _, _ = loop_state
        return jnp.logical_and(kb >= 0, worst > LOG2_WEIGHT_FLOOR)

    def step(loop_state):
        kb, _, acc, carry = loop_state
        new_acc, new_carry = [], []
        for c in chains:
            kt, v = key_block(c, kb)
            a, carry_c = _block_weights(nq[c], kt, tri, carry[c], None)
            new_acc.append(acc[c] + _dot(a, v))
            new_carry.append(carry_c)
        return kb - 1, slowest_decay(new_carry), tuple(new_acc), tuple(new_carry)

    _, _, acc, _ = lax.while_loop(more_to_do, step,
                                  (qi - 1 - P, slowest_decay(carry), acc, carry))
    for c in chains:
        out = acc[c][0:T]
        for h in range(1, G):
            out = jnp.where(head_of_lane == h, acc[c][h * T:(h + 1) * T], out)
        o_ref[0, :, c * W:(c + 1) * W] = out


def _attn_call(q, kt, v, tri):
    B, S, D = q.shape
    T, W = ATTN_TILE, ATTN_CHAINS * ATTN_HEADS * HEAD_DIM
    return pl.pallas_call(
        _attn_kernel,
        grid=(B, D // W, S // T),
        in_specs=[
            pl.BlockSpec((1, T, W), lambda b, p, i: (b, i, p)),
            pl.BlockSpec((1, S // T, W, T), lambda b, p, i: (b, 0, p, 0)),
            pl.BlockSpec((1, S, W), lambda b, p, i: (b, 0, p)),
            _const_spec((2 * T, T)),
        ],
        out_specs=pl.BlockSpec((1, T, W), lambda b, p, i: (b, i, p)),
        out_shape=jax.ShapeDtypeStruct((B, S, D), F32),
        compiler_params=pltpu.CompilerParams(
            dimension_semantics=("arbitrary", "arbitrary", "arbitrary"),
            vmem_limit_bytes=VMEM_LIMIT_BYTES),
        name="stickbreak_attn",
    )(q, kt, v, tri)


def _ffn_tile(x1, g_ref, wgu_ref, wd_ref, act_ref):
    h = _rmsnorm_rows(x1, g_ref[...]).astype(BF16)
    for c in range(D_FF // FF_CHUNK):
        lo = c * FF_CHUNK
        gate = _dot(h, wgu_ref[:, lo:lo + FF_CHUNK])
        up = _dot(h, wgu_ref[:, D_FF + lo:D_FF + lo + FF_CHUNK])
        act = gate / (1.0 + jnp.exp(-gate)) * up
        act_ref[:, lo:lo + FF_CHUNK] = act.astype(BF16)
    return x1 + _dot(act_ref[...], wd_ref[...])


def _attn_tail_kernel(x_ref, o_ref, wo_ref, g_ref, wgu_ref, wd_ref, out_ref, act_ref):
    rows_per_block = o_ref.shape[2]
    o = jnp.concatenate([o_ref[0, :, r, :] for r in range(rows_per_block)], axis=0)
    x1 = x_ref[0] + _dot(o.astype(BF16), wo_ref[...])
    out_ref[0] = _ffn_tile(x1, g_ref, wgu_ref, wd_ref, act_ref)


def _attn_tail_call(x, o, w_o, gain, w_gu, w_down):
    B, S, D = x.shape
    n_qblk = S // Q_BLOCK
    assert TOKEN_TILE % n_qblk == 0 and (TOKEN_TILE // n_qblk) % SUBLANES == 0
    row_spec = pl.BlockSpec((1, TOKEN_TILE, D), lambda b, i: (b, i, 0))
    return pl.pallas_call(
        _attn_tail_kernel,
        grid=(B, S // TOKEN_TILE),
        in_specs=[row_spec,
                  pl.BlockSpec((1, n_qblk, TOKEN_TILE // n_qblk, D), lambda b, i: (b, 0, i, 0)),
                  _const_spec((D, D)), _const_spec((1, D)),
                  _const_spec((D, 2 * D_FF)), _const_spec((D_FF, D))],
        out_specs=row_spec,
        out_shape=jax.ShapeDtypeStruct((B, S, D), F32),
        scratch_shapes=[pltpu.VMEM((TOKEN_TILE, D_FF), BF16)],
        compiler_params=pltpu.CompilerParams(
            dimension_semantics=("arbitrary", "arbitrary"),
            vmem_limit_bytes=VMEM_LIMIT_BYTES),
        name="outproj_ffn",
    )(x, o.reshape(B, n_qblk, Q_BLOCK, D), w_o, gain, w_gu, w_down)


def _pool_tail_kernel(x_ref, halo_ref, mg_ref, pw_ref, ps_ref, g_ref, wgu_ref, wd_ref,
                      out_ref, hist_ref, act_ref):
    si = pl.program_id(1)
    T = TOKEN_TILE
    x = x_ref[0]
    h = _rmsnorm_rows(x, mg_ref[...])
    h_halo = _rmsnorm_rows(halo_ref[0], mg_ref[...])
    hist_ref[0:POOL_HALO, :] = jnp.where(si > 0, h_halo, 0.0)
    hist_ref[POOL_HALO:POOL_HALO + T, :] = h

    pos = si * T + lax.broadcasted_iota(jnp.int32, (T, 1), 0)
    mixed = []
    for g, w in enumerate(POOL_WINDOWS):
        lo = g * POOL_GROUP_DIM
        acc = h[:, lo:lo + POOL_GROUP_DIM]
        for back in range(1, w):
            acc = acc + hist_ref[POOL_HALO - back:POOL_HALO - back + T, lo:lo + POOL_GROUP_DIM]
        count = jnp.minimum(pos + 1, w).astype(F32)
        pooled = acc / count - h[:, lo:lo + POOL_GROUP_DIM]
        mixed.append(_dot(pooled.astype(BF16), pw_ref[g]))
    x1 = x + jnp.concatenate(mixed, axis=-1) * ps_ref[...]
    out_ref[0] = _ffn_tile(x1, g_ref, wgu_ref, wd_ref, act_ref)


def _pool_tail_call(x, mix_gain, pool_w, pool_scale, gain, w_gu, w_down):
    B, S, D = x.shape
    T = TOKEN_TILE
    halo_blocks = T // POOL_HALO
    return pl.pallas_call(
        _pool_tail_kernel,
        grid=(B, S // T),
        in_specs=[
            pl.BlockSpec((1, T, D), lambda b, i: (b, i, 0)),
            pl.BlockSpec((1, POOL_HALO, D),
                         lambda b, i: (b, jnp.maximum(i * halo_blocks - 1, 0), 0)),
            _const_spec((1, D)),
            _const_spec(pool_w.shape),
            _const_spec((1, D)),
            _const_spec((1, D)),
            _const_spec((D, 2 * D_FF)),
            _const_spec((D_FF, D)),
        ],
        out_specs=pl.BlockSpec((1, T, D), lambda b, i: (b, i, 0)),
        out_shape=jax.ShapeDtypeStruct((B, S, D), F32),
        scratch_shapes=[pltpu.VMEM((POOL_HALO + T, D), F32),
                        pltpu.VMEM((T, D_FF), BF16)],
        compiler_params=pltpu.CompilerParams(
            dimension_semantics=("arbitrary", "arbitrary"),
            vmem_limit_bytes=VMEM_LIMIT_BYTES),
        name="pool_ffn",
    )(x, x, mix_gain, pool_w, pool_scale, gain, w_gu, w_down)


def kernel(x, mix_norm, ffn_norm, attn_w_qkv, attn_q_norm, attn_k_norm, attn_w_o,
           pool_w, pool_scale, ffn_w_gu, ffn_w_down):
    B, S, D = x.shape
    depth = mix_norm.shape[0]
    assert D == D_MODEL and S % TOKEN_TILE == 0 and TOKEN_TILE % ATTN_TILE == 0

    idx = jnp.arange(MXU_DIM)
    seg = (idx[:, None] // HEAD_DIM == idx[None, :] // HEAD_DIM).astype(BF16)
    tri = _stacked_tri(ATTN_TILE)

    for i in range(depth):
        j = i // 2
        gain = ffn_norm[i][None, :]
        w_gu = ffn_w_gu[i].astype(BF16)
        w_down = ffn_w_down[i].astype(BF16)
        if i % 2 == 0:
            q_gain = jnp.tile(attn_q_norm[j], N_HEADS)[None, :] * (-LOG2_E * HEAD_DIM ** -0.5)
            k_gain = jnp.tile(attn_k_norm[j], N_HEADS)[None, :]
            q, kt, v = _qkv_call(x, mix_norm[i][None, :], attn_w_qkv[j].astype(BF16),
                                 q_gain, k_gain, seg)
            o = _attn_call(q, kt, v, tri)
            x = _attn_tail_call(x, o, attn_w_o[j].astype(BF16), gain, w_gu, w_down)
        else:
            x = _pool_tail_call(x, mix_norm[i][None, :], pool_w[j].astype(BF16),
                                pool_scale[j][None, :], gain, w_gu, w_down)
    return x
```

```python
import jax
import jax.numpy as jnp
from jax import lax
from jax.experimental import pallas as pl
from jax.experimental.pallas import tpu as pltpu

D_MODEL = 1024
N_HEADS = 16
HEAD_DIM = 64
D_FF = 2816
POOL_WINDOWS = (2, 4, 8, 16)
POOL_GROUP_DIM = D_MODEL // len(POOL_WINDOWS)
POOL_HALO = 16
EPS = 1e-6

Q_BLOCK = 128
LANES = 128
SUBLANES = 8
MXU_DIM = 256
VMEM_LIMIT_BYTES = 56 * 1024 * 1024

TOKEN_TILE = 512
ATTN_TILE = 128
ATTN_HEADS = 2
ATTN_CHAINS = 4
ATTN_PREV = 2
LOG2_WEIGHT_FLOOR = -152.0
NO_BLOCK_BIAS = -1.0e4
LOG2_E = 1.4426950408889634
FF_CHUNK = MXU_DIM

BF16 = jnp.bfloat16
F32 = jnp.float32


def _dot(a, b):
    return jnp.dot(a, b, preferred_element_type=F32)


def _rmsnorm_rows(x, gain):
    ms = jnp.mean(x * x, axis=-1, keepdims=True)
    return x * lax.rsqrt(ms + EPS) * gain


def _split_bf16(x):
    hi = x.astype(BF16)
    lo = (x - hi.astype(F32)).astype(BF16)
    return hi, lo


def _const_spec(shape):
    zeros = (0,) * len(shape)
    return pl.BlockSpec(shape, lambda *_: zeros)


def _head_meansq(t, seg):
    sq = (t * t).astype(BF16)
    outs = [_dot(sq[:, c * MXU_DIM:(c + 1) * MXU_DIM], seg) for c in range(D_MODEL // MXU_DIM)]
    return jnp.concatenate(outs, axis=-1) * (1.0 / HEAD_DIM)


def _qkv_kernel(x_ref, g_ref, w_ref, qg_ref, kg_ref, seg_ref, q_ref, kt_ref, v_ref):
    h = _rmsnorm_rows(x_ref[0], g_ref[...]).astype(BF16)
    seg = seg_ref[...]
    q = _dot(h, w_ref[:, 0:D_MODEL])
    q = q * lax.rsqrt(_head_meansq(q, seg) + EPS) * qg_ref[...]
    q_ref[0] = q.astype(BF16)
    k = _dot(h, w_ref[:, D_MODEL:2 * D_MODEL])
    k = k * lax.rsqrt(_head_meansq(k, seg) + EPS) * kg_ref[...]
    for j in range(TOKEN_TILE // ATTN_TILE):
        kt_ref[0, j] = k[j * ATTN_TILE:(j + 1) * ATTN_TILE, :].T.astype(BF16)
    v_ref[0] = _dot(h, w_ref[:, 2 * D_MODEL:3 * D_MODEL]).astype(BF16)


def _qkv_call(x, gain, w_qkv, q_gain, k_gain, seg):
    B, S, D = x.shape
    n_kblk = TOKEN_TILE // ATTN_TILE
    return pl.pallas_call(
        _qkv_kernel,
        grid=(B, S // TOKEN_TILE),
        in_specs=[
            pl.BlockSpec((1, TOKEN_TILE, D), lambda b, i: (b, i, 0)),
            _const_spec((1, D)),
            _const_spec((D, 3 * D)),
            _const_spec((1, D)),
            _const_spec((1, D)),
            _const_spec((MXU_DIM, MXU_DIM)),
        ],
        out_specs=[
            pl.BlockSpec((1, TOKEN_TILE, D), lambda b, i: (b, i, 0)),
            pl.BlockSpec((1, n_kblk, D, ATTN_TILE), lambda b, i: (b, i, 0, 0)),
            pl.BlockSpec((1, TOKEN_TILE, D), lambda b, i: (b, i, 0)),
        ],
        out_shape=[
            jax.ShapeDtypeStruct((B, S, D), BF16),
            jax.ShapeDtypeStruct((B, S // ATTN_TILE, D, ATTN_TILE), BF16),
            jax.ShapeDtypeStruct((B, S, D), BF16),
        ],
        compiler_params=pltpu.CompilerParams(
            dimension_semantics=("arbitrary", "arbitrary"),
            vmem_limit_bytes=VMEM_LIMIT_BYTES),
        name="qkv_proj",
    )(x, gain, w_qkv, q_gain, k_gain, seg)


def _log2_one_minus_sigmoid(nz):
    return jnp.minimum(nz, 0.0) - jnp.log2(1.0 + jnp.exp2(-jnp.abs(nz)))


def _block_weights(nq, kt, tri2, carry, mask):
    nz = _dot(nq, kt)
    l2 = _log2_one_minus_sigmoid(nz)
    if mask is not None:
        l2 = jnp.where(mask, l2, 0.0)
    hi, lo = _split_bf16(l2)
    c = _dot(jnp.concatenate([hi, lo], axis=1), tri2) + carry
    a = jnp.exp2(c - nz)
    if mask is not None:
        a = jnp.where(mask, a, 0.0)
    return a.astype(BF16), c[:, 0:1]


def _stacked_tri(width):
    idx = jnp.arange(width)
    tri = (idx[:, None] >= idx[None, :]).astype(BF16)
    return jnp.concatenate([tri, tri], axis=0)


def _attn_kernel(q_ref, kt_ref, v_ref, tri_ref, o_ref):
    qi = pl.program_id(2)
    T, G, P = ATTN_TILE, ATTN_HEADS, ATTN_PREV
    W = G * HEAD_DIM
    M = G * T
    chains = range(ATTN_CHAINS)
    head_of_lane = lax.broadcasted_iota(jnp.int32, (T, W), 1) // HEAD_DIM
    row = lax.broadcasted_iota(jnp.int32, (M, 1), 0)
    col = lax.broadcasted_iota(jnp.int32, (1, T), 1)
    mask = col < (row & (T - 1))
    tri = tri_ref[...]

    def stacked_q(c):
        q = q_ref[0, :, c * W:(c + 1) * W]
        zero = jnp.zeros_like(q)
        return jnp.concatenate([jnp.where(head_of_lane == h, q, zero) for h in range(G)], axis=0)

    def key_block(c, kb):
        return (kt_ref[0, kb, c * W:(c + 1) * W, :],
                v_ref[0, pl.ds(pl.multiple_of(kb * T, T), T), c * W:(c + 1) * W])

    nq = [stacked_q(c) for c in chains]

    blocks = [[key_block(c, jnp.maximum(qi - d, 0)) for d in range(P, -1, -1)] for c in chains]

    def scores(c):
        nz = _dot(nq[c], jnp.concatenate([kt for kt, _ in blocks[c]], axis=1))
        return [nz[:, j * T:(j + 1) * T] for j in range(P + 1)]

    def log_terms(nz):
        l2 = [_log2_one_minus_sigmoid(nz[j]) for j in range(P)]
        l2.append(jnp.where(mask, _log2_one_minus_sigmoid(nz[P]), 0.0))
        hi, lo = _split_bf16(jnp.concatenate(l2, axis=0))
        return jnp.concatenate([hi, lo], axis=1)

    def weights(nz, c):
        c = [c[j * M:(j + 1) * M] for j in range(P + 1)]
        a = [None] * P + [jnp.where(mask, jnp.exp2(c[P] - nz[P]), 0.0)]
        for j in range(P - 1, -1, -1):
            exists = qi >= P - j
            c[j] = c[j] + (c[j + 1][:, 0:1] + jnp.where(exists, 0.0, NO_BLOCK_BIAS))
            a[j] = jnp.exp2(c[j] - nz[j])
        return jnp.concatenate(a, axis=1).astype(BF16), c[0][:, 0:1]

    nz = [scores(c) for c in chains]
    lhs = [log_terms(nz[c]) for c in chains]
    csum = [_dot(lhs[c], tri) for c in chains]
    a_carry = [weights(nz[c], csum[c]) for c in chains]
    acc = tuple(_dot(a_carry[c][0], jnp.concatenate([v for _, v in blocks[c]], axis=0))
                for c in chains)
    carry = tuple(a_carry[c][1] for c in chains)

    def slowest_decay(carry):
        worst = carry[0]
        for other in carry[1:]:
            worst = jnp.maximum(worst, other)
        return jnp.max(worst)

    def more_to_do(loop_state):
        kb, worst, _, _ = loop_state
        return jnp.logical_and(kb >= 0, worst > LOG2_WEIGHT_FLOOR)

    def step(loop_state):
        kb, _, acc, carry = loop_state
        new_acc, new_carry = [], []
        for c in chains:
            kt, v = key_block(c, kb)
            a, carry_c = _block_weights(nq[c], kt, tri, carry[c], None)
            new_acc.append(acc[c] + _dot(a, v))
            new_carry.append(carry_c)
        return kb - 1, slowest_decay(new_carry), tuple(new_acc), tuple(new_carry)

    _, _, acc, _ = lax.while_loop(more_to_do, step,
                                  (qi - 1 - P, slowest_decay(carry), acc, carry))
    for c in chains:
        out = acc[c][0:T]
        for h in range(1, G):
            out = jnp.where(head_of_lane == h, acc[c][h * T:(h + 1) * T], out)
        o_ref[0, :, c * W:(c + 1) * W] = out


def _attn_call(q, kt, v, tri):
    B, S, D = q.shape
    T, W = ATTN_TILE, ATTN_CHAINS * ATTN_HEADS * HEAD_DIM
    return pl.pallas_call(
        _attn_kernel,
        grid=(B, D // W, S // T),
        in_specs=[
            pl.BlockSpec((1, T, W), lambda b, p, i: (b, i, p)),
            pl.BlockSpec((1, S // T, W, T), lambda b, p, i: (b, 0, p, 0)),
            pl.BlockSpec((1, S, W), lambda b, p, i: (b, 0, p)),
            _const_spec((2 * T, T)),
        ],
        out_specs=pl.BlockSpec((1, T, W), lambda b, p, i: (b, i, p)),
        out_shape=jax.ShapeDtypeStruct((B, S, D), F32),
        compiler_params=pltpu.CompilerParams(
            dimension_semantics=("arbitrary", "arbitrary", "arbitrary"),
            vmem_limit_bytes=VMEM_LIMIT_BYTES),
        name="stickbreak_attn",
    )(q, kt, v, tri)


def _ffn_tile(x1, g_ref, wgu_ref, wd_ref, act_ref):
    h = _rmsnorm_rows(x1, g_ref[...]).astype(BF16)
    for c in range(D_FF // FF_CHUNK):
        lo = c * FF_CHUNK
        gate = _dot(h, wgu_ref[:, lo:lo + FF_CHUNK])
        up = _dot(h, wgu_ref[:, D_FF + lo:D_FF + lo + FF_CHUNK])
        act = gate / (1.0 + jnp.exp(-gate)) * up
        act_ref[:, lo:lo + FF_CHUNK] = act.astype(BF16)
    return x1 + _dot(act_ref[...], wd_ref[...])


def _attn_tail_kernel(x_ref, o_ref, wo_ref, g_ref, wgu_ref, wd_ref, out_ref, act_ref):
    rows_per_block = o_ref.shape[2]
    o = jnp.concatenate([o_ref[0, :, r, :] for r in range(rows_per_block)], axis=0)
    x1 = x_ref[0] + _dot(o.astype(BF16), wo_ref[...])
    out_ref[0] = _ffn_tile(x1, g_ref, wgu_ref, wd_ref, act_ref)


def _attn_tail_call(x, o, w_o, gain, w_gu, w_down):
    B, S, D = x.shape
    n_qblk = S // Q_BLOCK
    assert TOKEN_TILE % n_qblk == 0 and (TOKEN_TILE // n_qblk) % SUBLANES == 0
    row_spec = pl.BlockSpec((1, TOKEN_TILE, D), lambda b, i: (b, i, 0))
    return pl.pallas_call(
        _attn_tail_kernel,
        grid=(B, S // TOKEN_TILE),
        in_specs=[row_spec,
                  pl.BlockSpec((1, n_qblk, TOKEN_TILE // n_qblk, D), lambda b, i: (b, 0, i, 0)),
                  _const_spec((D, D)), _const_spec((1, D)),
                  _const_spec((D, 2 * D_FF)), _const_spec((D_FF, D))],
        out_specs=row_spec,
        out_shape=jax.ShapeDtypeStruct((B, S, D), F32),
        scratch_shapes=[pltpu.VMEM((TOKEN_TILE, D_FF), BF16)],
        compiler_params=pltpu.CompilerParams(
            dimension_semantics=("arbitrary", "arbitrary"),
            vmem_limit_bytes=VMEM_LIMIT_BYTES),
        name="outproj_ffn",
    )(x, o.reshape(B, n_qblk, Q_BLOCK, D), w_o, gain, w_gu, w_down)


def _pool_tail_kernel(x_ref, halo_ref, mg_ref, pw_ref, ps_ref, g_ref, wgu_ref, wd_ref,
                      out_ref, hist_ref, act_ref):
    si = pl.program_id(1)
    T = TOKEN_TILE
    x = x_ref[0]
    h = _rmsnorm_rows(x, mg_ref[...])
    h_halo = _rmsnorm_rows(halo_ref[0], mg_ref[...])
    hist_ref[0:POOL_HALO, :] = jnp.where(si > 0, h_halo, 0.0)
    hist_ref[POOL_HALO:POOL_HALO + T, :] = h

    pos = si * T + lax.broadcasted_iota(jnp.int32, (T, 1), 0)
    mixed = []
    for g, w in enumerate(POOL_WINDOWS):
        lo = g * POOL_GROUP_DIM
        acc = h[:, lo:lo + POOL_GROUP_DIM]
        for back in range(1, w):
            acc = acc + hist_ref[POOL_HALO - back:POOL_HALO - back + T, lo:lo + POOL_GROUP_DIM]
        count = jnp.minimum(pos + 1, w).astype(F32)
        pooled = acc / count - h[:, lo:lo + POOL_GROUP_DIM]
        mixed.append(_dot(pooled.astype(BF16), pw_ref[g]))
    x1 = x + jnp.concatenate(mixed, axis=-1) * ps_ref[...]
    out_ref[0] = _ffn_tile(x1, g_ref, wgu_ref, wd_ref, act_ref)


def _pool_tail_call(x, mix_gain, pool_w, pool_scale, gain, w_gu, w_down):
    B, S, D = x.shape
    T = TOKEN_TILE
    halo_blocks = T // POOL_HALO
    return pl.pallas_call(
        _pool_tail_kernel,
        grid=(B, S // T),
        in_specs=[
            pl.BlockSpec((1, T, D), lambda b, i: (b, i, 0)),
            pl.BlockSpec((1, POOL_HALO, D),
                         lambda b, i: (b, jnp.maximum(i * halo_blocks - 1, 0), 0)),
            _const_spec((1, D)),
            _const_spec(pool_w.shape),
            _const_spec((1, D)),
            _const_spec((1, D)),
            _const_spec((D, 2 * D_FF)),
            _const_spec((D_FF, D)),
        ],
        out_specs=pl.BlockSpec((1, T, D), lambda b, i: (b, i, 0)),
        out_shape=jax.ShapeDtypeStruct((B, S, D), F32),
        scratch_shapes=[pltpu.VMEM((POOL_HALO + T, D), F32),
                        pltpu.VMEM((T, D_FF), BF16)],
        compiler_params=pltpu.CompilerParams(
            dimension_semantics=("arbitrary", "arbitrary"),
            vmem_limit_bytes=VMEM_LIMIT_BYTES),
        name="pool_ffn",
    )(x, x, mix_gain, pool_w, pool_scale, gain, w_gu, w_down)


def kernel(x, mix_norm, ffn_norm, attn_w_qkv, attn_q_norm, attn_k_norm, attn_w_o,
           pool_w, pool_scale, ffn_w_gu, ffn_w_down):
    B, S, D = x.shape
    depth = mix_norm.shape[0]
    assert D == D_MODEL and S % TOKEN_TILE == 0 and TOKEN_TILE % ATTN_TILE == 0

    idx = jnp.arange(MXU_DIM)
    seg = (idx[:, None] // HEAD_DIM == idx[None, :] // HEAD_DIM).astype(BF16)
    tri = _stacked_tri(ATTN_TILE)

    for i in range(depth):
        j = i // 2
        gain = ffn_norm[i][None, :]
        w_gu = ffn_w_gu[i].astype(BF16)
        w_down = ffn_w_down[i].astype(BF16)
        if i % 2 == 0:
            q_gain = jnp.tile(attn_q_norm[j], N_HEADS)[None, :] * (-LOG2_E * HEAD_DIM ** -0.5)
            k_gain = jnp.tile(attn_k_norm[j], N_HEADS)[None, :]
            q, kt, v = _qkv_call(x, mix_norm[i][None, :], attn_w_qkv[j].astype(BF16),
                                 q_gain, k_gain, seg)
            o = _attn_call(q, kt, v, tri)
            x = _attn_tail_call(x, o, attn_w_o[j].astype(BF16), gain, w_gu, w_down)
        else:
            x = _pool_tail_call(x, mix_norm[i][None, :], pool_w[j].astype(BF16),
                                pool_scale[j][None, :], gain, w_gu, w_down)
    return x
```
